```python
import jax, jax.numpy as jnp
from jax import lax
import numpy as np

D_MODEL = 1024
BATCH = 2
SEQ = 8192
DEPTH = 1
DEC_BATCH = 8
DEC_SEQ = 32
PAST_LEN = 2048

CHUNK = 64
A_HEADS = 8
A_DK = 128
A_DV = 128
A_WIDTH = A_HEADS * A_DK
POOL_WINDOWS = (2, 4, 8, 16)
POOL_GROUPS = 4
POOL_GDIM = 128
B_WIDTH = POOL_GROUPS * POOL_GDIM
POOL_BUF = max(POOL_WINDOWS) - 1
D_FF = 2816
PLE_DIM = 256
IN_WIDTH = 4 * A_WIDTH + B_WIDTH + 2 * D_MODEL
IN_SPLITS = (A_WIDTH, 2 * A_WIDTH, 3 * A_WIDTH, 4 * A_WIDTH,
             4 * A_WIDTH + B_WIDTH, 4 * A_WIDTH + B_WIDTH + D_MODEL)
N_NORMS = 8
EPS = 1e-6

kernel_name = 'hgrn2_pool_macaron_streaming_step'


def rmsnorm(x, g):
    xf = x.astype(jnp.float32)
    y = xf * lax.rsqrt(jnp.mean(xf * xf, axis=-1, keepdims=True) + EPS)
    return (y * g.astype(jnp.float32)).astype(x.dtype)


def swiglu(h, w_gu, w_down):
    a, b = jnp.split(h @ w_gu, 2, axis=-1)
    return (jax.nn.silu(a) * b) @ w_down


def hgrn_chunked(q, k, v, logf, s0, chunk):
    bsz, t_len, n_h, _ = q.shape
    n_c = t_len // chunk

    def to_chunks(a):
        return a.reshape(bsz, n_c, chunk, n_h, a.shape[-1]).transpose(1, 0, 3, 2, 4)

    mask = jnp.tril(jnp.ones((chunk, chunk), dtype=bool))[:, :, None]

    def step(S, inp):
        qc, kc, vc, gc = inp
        G = jnp.cumsum(gc, axis=-2)
        o_inter = jnp.einsum('bhtk,bhkv->bhtv', qc * jnp.exp(G), S)
        diff = G[:, :, :, None, :] - G[:, :, None, :, :]
        decay = jnp.exp(jnp.where(mask, diff, -jnp.inf))
        att = jnp.einsum('bhtk,bhsk,bhtsk->bhts', qc, kc, decay)
        o = o_inter + jnp.einsum('bhts,bhsv->bhtv', att, vc)
        G_last = G[:, :, -1:, :]
        S_new = (jnp.exp(G_last[:, :, 0, :])[..., None] * S
                 + jnp.einsum('bhsk,bhsv->bhkv', kc * jnp.exp(G_last - G), vc))
        return S_new, o

    S, o = lax.scan(step, s0, (to_chunks(q), to_chunks(k), to_chunks(v), to_chunks(logf)))
    o = o.transpose(1, 0, 3, 2, 4).reshape(bsz, t_len, n_h, v.shape[-1])
    return o, S


def pool_mix(u, buf, start_pos, w_pool, scale):
    bsz, t_len, width = u.shape
    uf = u.astype(jnp.float32)
    xp = jnp.concatenate([buf.astype(jnp.float32), uf], axis=1)
    c = jnp.cumsum(jnp.pad(xp, ((0, 0), (1, 0), (0, 0))), axis=1)
    end = c[:, POOL_BUF + 1:]
    pos = start_pos + jnp.arange(t_len)
    groups = []
    for g, w in enumerate(POOL_WINDOWS):
        sl = slice(g * POOL_GDIM, (g + 1) * POOL_GDIM)
        start = c[:, POOL_BUF + 1 - w: POOL_BUF + 1 - w + t_len, sl]
        cnt = jnp.minimum(w, pos + 1).astype(jnp.float32)[None, :, None]
        groups.append((end[..., sl] - start) / cnt - uf[..., sl])
    pooled = jnp.stack(groups, axis=2)
    mixed = jnp.einsum('btgc,gcd->btgd', pooled, w_pool.astype(jnp.float32))
    mixed = mixed.reshape(bsz, t_len, width) * scale.astype(jnp.float32)
    return mixed.astype(u.dtype), xp[:, -POOL_BUF:].astype(u.dtype)


def layer(x, p, s_hgrn, s_pool, start_pos, chunk, lb, gains,
          w_ffn1_gu, w_ffn1_down, w_in, a_gnorm, w_a_up, w_pool, pool_scale,
          w_b_up, w_out, w_ffn2_gu, w_ffn2_down, w_ple_proj, w_ple_gate):
    bsz, t_len, _ = x.shape
    x = x + 0.5 * rmsnorm(swiglu(rmsnorm(x, gains[0]), w_ffn1_gu, w_ffn1_down), gains[1])
    h = rmsnorm(x, gains[2])
    z = h @ w_in
    q, fz, iv, og, u, ga, gb = jnp.split(z, IN_SPLITS, axis=-1)
    lb_h = lb.reshape(A_HEADS, A_DK)
    fz = fz.astype(jnp.float32).reshape(bsz, t_len, A_HEADS, A_DK)
    logf = jnp.logaddexp(jnp.log(lb_h), jnp.log1p(-lb_h) + jax.nn.log_sigmoid(fz))
    k = (1.0 - lb_h) * jax.nn.sigmoid(-fz)
    qh = jax.nn.silu(q.astype(jnp.float32)).reshape(bsz, t_len, A_HEADS, A_DK)
    vh = iv.astype(jnp.float32).reshape(bsz, t_len, A_HEADS, A_DV)
    o, s_new = hgrn_chunked(qh, k, vh, logf, s_hgrn.astype(jnp.float32), chunk)
    o = o * lax.rsqrt(jnp.mean(o * o, axis=-1, keepdims=True) + EPS) * a_gnorm.astype(jnp.float32)
    o = o.reshape(bsz, t_len, A_WIDTH).astype(x.dtype) * jax.nn.silu(og)
    y_a = o @ w_a_up
    pooled, buf_new = pool_mix(u, s_pool, start_pos, w_pool, pool_scale)
    y_b = pooled @ w_b_up
    m = jax.nn.sigmoid(ga) * y_a + jax.nn.sigmoid(gb) * y_b
    x = x + rmsnorm(m @ w_out, gains[3])
    x = x + 0.5 * rmsnorm(swiglu(rmsnorm(x, gains[4]), w_ffn2_gu, w_ffn2_down), gains[5])
    e = p.astype(x.dtype) @ w_ple_proj
    gate = jax.nn.sigmoid(rmsnorm(x, gains[6]) @ w_ple_gate)
    x = x + rmsnorm(gate * e, gains[7])
    return x, s_new, buf_new


def setup_inputs(seed: int = 0) -> dict:
    key = jax.random.key(seed)
    ks = jax.random.split(key, 24)

    def nrm(k, shape, scale):
        return jax.random.normal(k, shape, jnp.float32) * scale

    return {
        'x_prompt': nrm(ks[0], (BATCH, SEQ, D_MODEL), 1.0),
        'x_sample': nrm(ks[1], (DEC_BATCH, DEC_SEQ, D_MODEL), 1.0),
        'p_prompt': nrm(ks[2], (DEPTH, BATCH, SEQ, PLE_DIM), 1.0),
        'p_sample': nrm(ks[3], (DEPTH, DEC_BATCH, DEC_SEQ, PLE_DIM), 1.0),
        'state_hgrn': nrm(ks[4], (DEPTH, DEC_BATCH, A_HEADS, A_DK, A_DV), 0.5),
        'state_pool': nrm(ks[5], (DEPTH, DEC_BATCH, POOL_BUF, B_WIDTH), 1.0),
        'norm_gains': 1.0 + nrm(ks[6], (DEPTH, N_NORMS, D_MODEL), 0.02),
        'lb_logits': nrm(ks[7], (DEPTH + 1, A_WIDTH), 0.1),
        'w_ffn1_gu': nrm(ks[8], (DEPTH, D_MODEL, 2 * D_FF), D_MODEL ** -0.5),
        'w_ffn1_down': nrm(ks[9], (DEPTH, D_FF, D_MODEL), D_FF ** -0.5),
        'w_in': nrm(ks[10], (DEPTH, D_MODEL, IN_WIDTH), D_MODEL ** -0.5),
        'a_gnorm': 1.0 + nrm(ks[11], (DEPTH, A_DV), 0.02),
        'w_a_up': nrm(ks[12], (DEPTH, A_WIDTH, D_MODEL), A_WIDTH ** -0.5),
        'w_pool': nrm(ks[13], (DEPTH, POOL_GROUPS, POOL_GDIM, POOL_GDIM), POOL_GDIM ** -0.5),
        'pool_scale': 1.0 + nrm(ks[14], (DEPTH, B_WIDTH), 0.1),
        'w_b_up': nrm(ks[15], (DEPTH, B_WIDTH, D_MODEL), B_WIDTH ** -0.5),
        'w_out': nrm(ks[16], (DEPTH, D_MODEL, D_MODEL), D_MODEL ** -0.5),
        'w_ffn2_gu': nrm(ks[17], (DEPTH, D_MODEL, 2 * D_FF), D_MODEL ** -0.5),
        'w_ffn2_down': nrm(ks[18], (DEPTH, D_FF, D_MODEL), D_FF ** -0.5),
        'w_ple_proj': nrm(ks[19], (DEPTH, PLE_DIM, D_MODEL), PLE_DIM ** -0.5),
        'w_ple_gate': nrm(ks[20], (DEPTH, D_MODEL, D_MODEL), D_MODEL ** -0.5),
    }


def reference(x_prompt, x_sample, p_prompt, p_sample, state_hgrn, state_pool,
              norm_gains, lb_logits, w_ffn1_gu, w_ffn1_down, w_in, a_gnorm, w_a_up,
              w_pool, pool_scale, w_b_up, w_out, w_ffn2_gu, w_ffn2_down,
              w_ple_proj, w_ple_gate):
    lb_all = jnp.cumsum(jax.nn.softmax(lb_logits.astype(jnp.float32), axis=0), axis=0)
    bsz_p = x_prompt.shape[0]
    dec_len = x_sample.shape[1]
    xp, xs = x_prompt, x_sample
    hp_list, bp_list, hs_list, bs_list = [], [], [], []
    for l in range(DEPTH):
        weights = (w_ffn1_gu[l], w_ffn1_down[l], w_in[l], a_gnorm[l], w_a_up[l], w_pool[l],
                   pool_scale[l], w_b_up[l], w_out[l], w_ffn2_gu[l], w_ffn2_down[l],
                   w_ple_proj[l], w_ple_gate[l])
        s0 = jnp.zeros((bsz_p, A_HEADS, A_DK, A_DV), jnp.float32)
        b0 = jnp.zeros((bsz_p, POOL_BUF, B_WIDTH), xp.dtype)
        xp, hp, bp = layer(xp, p_prompt[l], s0, b0, 0, CHUNK, lb_all[l], norm_gains[l], *weights)
        xs, hs, bs = layer(xs, p_sample[l], state_hgrn[l], state_pool[l], PAST_LEN, dec_len,
                           lb_all[l], norm_gains[l], *weights)
        hp_list.append(hp)
        bp_list.append(bp)
        hs_list.append(hs)
        bs_list.append(bs)
    new_hgrn_prompt = jnp.stack(hp_list)
    new_pool_prompt = jnp.stack(bp_list)
    new_hgrn_sample = jnp.stack(hs_list)
    new_pool_sample = jnp.stack(bs_list)
    return (xp, xs, new_hgrn_prompt, new_pool_prompt, new_hgrn_sample, new_pool_sample)
```

```python
import functools

import jax
import jax.numpy as jnp
from jax import lax
from jax.experimental import pallas as pl
from jax.experimental.pallas import tpu as pltpu

EPS = 1e-6
HEAD_DIM = 128
POOL_WINDOWS = (2, 4, 8, 16)
POOL_GDIM = 128
POOL_HIST = 16
PROMPT_CHUNK = 64
V7X_VMEM_LIMIT_BYTES = 56 * 1024 * 1024

BF16 = jnp.bfloat16
F32 = jnp.float32


def _dot(a, b):
    return jnp.dot(a, b, preferred_element_type=F32)


def _dot_nt(a, b):
    return lax.dot_general(a, b, (((1,), (1,)), ((), ())), preferred_element_type=F32)


def _dot_tn(a, b):
    return lax.dot_general(a, b, (((0,), (0,)), ((), ())), preferred_element_type=F32)


def _rmsnorm(x, g):
    ms = jnp.mean(x * x, axis=-1, keepdims=True)
    return x * lax.rsqrt(ms + EPS) * g


def _silu(x):
    return x * jax.nn.sigmoid(x)


def _resident(shape):
    zeros = (0,) * len(shape)
    return pl.BlockSpec(shape, lambda *_: zeros, pipeline_mode=pl.Buffered(1))


def _ffn_kernel(*refs, d_ff, ff_chunk, pre, post, ple):
    if ple is None:
        x_ref, gains_ref, wgu_ref, wd_ref, o_ref, act_ref = refs
    else:
        x_ref, p_ref, gains_ref, wgu_ref, wd_ref, wpp_ref, wpg_ref, o_ref, act_ref = refs
    x = x_ref[...]
    h = _rmsnorm(x, gains_ref[pre:pre + 1, :]).astype(BF16)
    for c in range(d_ff // ff_chunk):
        lo = c * ff_chunk
        a = _dot(h, wgu_ref[:, lo:lo + ff_chunk])
        b = _dot(h, wgu_ref[:, d_ff + lo:d_ff + lo + ff_chunk])
        act_ref[:, lo:lo + ff_chunk] = (_silu(a) * b).astype(BF16)
    y = _dot(act_ref[...], wd_ref[...])
    x = x + 0.5 * _rmsnorm(y, gains_ref[post:post + 1, :])
    if ple is not None:
        g_pre, g_post = ple
        e = _dot(p_ref[...].astype(BF16), wpp_ref[...])
        gate = jax.nn.sigmoid(_dot(_rmsnorm(x, gains_ref[g_pre:g_pre + 1, :]).astype(BF16), wpg_ref[...]))
        x = x + _rmsnorm(gate * e, gains_ref[g_post:g_post + 1, :])
    o_ref[...] = x


def _ffn(x, gains, w_gu, w_down, *, pre, post, p=None, w_pp=None, w_pg=None, ple=None):
    n, d = x.shape
    d_ff = w_down.shape[0]
    tm = min(512, n)
    assert n % tm == 0
    ff_chunk = 256
    assert d_ff % ff_chunk == 0
    row = lambda width: pl.BlockSpec((tm, width), lambda i: (i, 0))
    in_specs = [row(d)]
    args = [x]
    if ple is not None:
        in_specs.append(row(p.shape[1]))
        args.append(p)
    in_specs += [_resident(gains.shape), _resident(w_gu.shape), _resident(w_down.shape)]
    args += [gains, w_gu, w_down]
    if ple is not None:
        in_specs += [_resident(w_pp.shape), _resident(w_pg.shape)]
        args += [w_pp, w_pg]
    return pl.pallas_call(
        functools.partial(_ffn_kernel, d_ff=d_ff, ff_chunk=ff_chunk, pre=pre, post=post, ple=ple),
        grid=(n // tm,),
        in_specs=in_specs,
        out_specs=row(d),
        out_shape=jax.ShapeDtypeStruct((n, d), F32),
        scratch_shapes=[pltpu.VMEM((tm, d_ff), BF16)],
        compiler_params=pltpu.CompilerParams(
            dimension_semantics=("arbitrary",), vmem_limit_bytes=V7X_VMEM_LIMIT_BYTES),
        name="ffn_ple" if ple is not None else "ffn",
    )(*args)


def _mixer_kernel(*refs, layer, n_heads, tile, chunk, seq_rows, carry, start_pos, g_pre, g_post):
    if carry:
        (x_ref, gains_ref, lbl_ref, win_ref, agn_ref, waup_ref, wpool_ref, pscale_ref, wbup_ref,
         wout_ref, o_ref, s_out_ref, pool_out_ref,
         st_scr, hist_scr, ext_scr, gated_scr, mixed_scr) = refs
    else:
        (x_ref, gains_ref, lbl_ref, win_ref, agn_ref, waup_ref, wpool_ref, pscale_ref, wbup_ref,
         wout_ref, s0_ref, hist0_ref, o_ref, s_out_ref, pool_out_ref,
         st_scr, ext_scr, gated_scr, mixed_scr) = refs
    a_width = n_heads * HEAD_DIM
    b_width = len(POOL_WINDOWS) * POOL_GDIM
    d_model = x_ref.shape[-1]
    n_seq = tile // seq_rows
    n_chunks = tile // chunk
    ext_rows = POOL_HIST + seq_rows
    step = pl.program_id(1) if carry else 0

    if carry:
        @pl.when(step == 0)
        def _():
            st_scr[...] = jnp.zeros_like(st_scr)
            hist_scr[...] = jnp.zeros_like(hist_scr)
    else:
        for s in range(n_seq):
            for hd in range(n_heads):
                st_scr[s * n_heads + hd] = s0_ref[s, hd].T

    x = x_ref[...]
    h = _rmsnorm(x, gains_ref[g_pre:g_pre + 1, :]).astype(BF16)

    logits = lbl_ref[...]
    ex = jnp.exp(logits - jnp.max(logits, axis=0, keepdims=True))
    lb = jnp.sum(ex[:layer + 1], axis=0, keepdims=True) / jnp.sum(ex, axis=0, keepdims=True)

    r_idx = lax.broadcasted_iota(jnp.int32, (tile, tile), 0)
    c_idx = lax.broadcasted_iota(jnp.int32, (tile, tile), 1)
    causal = (c_idx <= r_idx) & (c_idx >= (r_idx // chunk) * chunk)
    tri = jnp.where(causal, 1.0, 0.0).astype(BF16)

    pair = 2 * HEAD_DIM
    for hp in range(n_heads // 2):
        col = lambda seg: slice(seg * a_width + hp * pair, seg * a_width + (hp + 1) * pair)
        q = _dot(h, win_ref[:, col(0)])
        fz = _dot(h, win_ref[:, col(1)])
        v = _dot(h, win_ref[:, col(2)]).astype(BF16)
        og = _dot(h, win_ref[:, col(3)])
        lb2 = lb[:, hp * pair:(hp + 1) * pair]
        e = jnp.exp(-jnp.abs(fz))
        r = 1.0 / (1.0 + e)
        pos = fz >= 0.0
        sig_p = jnp.where(pos, r, e * r)
        sig_n = jnp.where(pos, e * r, r)
        logf = jnp.log(lb2 + (1.0 - lb2) * sig_p)
        kk = (1.0 - lb2) * sig_n
        hi = logf.astype(BF16)
        lo = (logf - hi.astype(F32)).astype(BF16)
        g_cum = _dot(tri, hi) + _dot(tri, lo)
        dec = jnp.exp(g_cum)
        q_dec = (_silu(q) * dec).astype(BF16)
        k_inv = kk * jnp.exp(-g_cum)
        k_inv_bf = k_inv.astype(BF16)
        gated_og = _silu(og)
        for sub in range(2):
            hd = hp * 2 + sub
            ls = slice(sub * HEAD_DIM, (sub + 1) * HEAD_DIM)
            att = _dot_nt(q_dec[:, ls], k_inv_bf[:, ls])
            att = jnp.where(causal, att, 0.0).astype(BF16)
            o_intra = _dot(att, v[:, ls])
            o_parts = []
            for c in range(n_chunks):
                rows = slice(c * chunk, (c + 1) * chunk)
                s_idx = hd if carry else (c * chunk // seq_rows) * n_heads + hd
                st = st_scr[s_idx]
                o_parts.append(o_intra[rows] + _dot_nt(q_dec[rows, ls], st.astype(BF16)))
                d_last = dec[(c + 1) * chunk - 1:(c + 1) * chunk, ls]
                k_end = (k_inv[rows, ls] * d_last).astype(BF16)
                st_scr[s_idx] = st * d_last + _dot_tn(v[rows, ls], k_end)
            o = jnp.concatenate(o_parts, axis=0) if n_chunks > 1 else o_parts[0]
            o = o * lax.rsqrt(jnp.mean(o * o, axis=-1, keepdims=True) + EPS) * agn_ref[...]
            gated_scr[:, hd * HEAD_DIM:(hd + 1) * HEAD_DIM] = (o * gated_og[:, ls]).astype(BF16)
    y_a = _dot(gated_scr[...], waup_ref[...])

    u = _dot(h, win_ref[:, 4 * a_width:4 * a_width + b_width])
    row = lax.broadcasted_iota(jnp.int32, (seq_rows, 1), 0)
    first_pos = start_pos + (step * tile if carry else 0)
    for s in range(n_seq):
        base = s * ext_rows
        if carry:
            ext_scr[base:base + POOL_HIST, :] = hist_scr[...]
        else:
            ext_scr[base + 1:base + POOL_HIST, :] = hist0_ref[s]
        ext_scr[base + POOL_HIST:base + ext_rows, :] = u[s * seq_rows:(s + 1) * seq_rows]
    for s in range(n_seq):
        base = s * ext_rows
        for g, w in enumerate(POOL_WINDOWS):
            cols = slice(g * POOL_GDIM, (g + 1) * POOL_GDIM)
            cur = ext_scr[base + POOL_HIST:base + ext_rows, cols]
            acc = cur
            for j in range(1, w):
                acc = acc + ext_scr[base + POOL_HIST - j:base + ext_rows - j, cols]
            cnt = jnp.minimum(w, first_pos + row + 1).astype(F32)
            pooled = acc / cnt - cur
            mixed = _dot(pooled.astype(BF16), wpool_ref[g]) * pscale_ref[:, cols]
            mixed_scr[s * seq_rows:(s + 1) * seq_rows, cols] = mixed.astype(BF16)
        last = ext_scr[base + seq_rows:base + ext_rows, :]
        if carry:
            hist_scr[...] = last
            pool_out_ref[...] = last
        else:
            pool_out_ref[s] = last
    y_b = _dot(mixed_scr[...], wbup_ref[...])

    ga = _dot(h, win_ref[:, 4 * a_width + b_width:4 * a_width + b_width + d_model])
    gb = _dot(h, win_ref[:, 4 * a_width + b_width + d_model:4 * a_width + b_width + 2 * d_model])
    m = (jax.nn.sigmoid(ga) * y_a + jax.nn.sigmoid(gb) * y_b).astype(BF16)
    o_ref[...] = x + _rmsnorm(_dot(m, wout_ref[...]), gains_ref[g_post:g_post + 1, :])

    if carry:
        @pl.when(step == pl.num_programs(1) - 1)
        def _():
            for hd in range(n_heads):
                s_out_ref[hd] = st_scr[hd].T
    else:
        for s in range(n_seq):
            for hd in range(n_heads):
                s_out_ref[s, hd] = st_scr[s * n_heads + hd].T


def _mixer(x, gains, lb_logits, w_in, a_gnorm, w_a_up, w_pool, pool_scale, w_b_up, w_out,
           *, layer, state=None, hist=None, start_pos, g_pre, g_post):
    bsz, t_len, d = x.shape
    n_heads = w_a_up.shape[0] // HEAD_DIM
    b_width = w_b_up.shape[0]
    carry = state is None
    weights = [gains, lb_logits, w_in, a_gnorm, w_a_up, w_pool, pool_scale, w_b_up, w_out]
    w_specs = [_resident(w.shape) for w in weights]
    if carry:
        tile = min(256, t_len)
        chunk = min(PROMPT_CHUNK, tile)
        assert t_len % tile == 0 and tile % chunk == 0 and tile >= POOL_HIST
        seq_rows, n_seq, n_state = tile, 1, n_heads
        grid = (bsz, t_len // tile)
        x_in = x
        in_specs = [pl.BlockSpec((None, tile, d), lambda b, t: (b, t, 0))] + w_specs
        args = [x_in] + weights
        out_specs = [
            pl.BlockSpec((None, tile, d), lambda b, t: (b, t, 0)),
            pl.BlockSpec((None, n_heads, HEAD_DIM, HEAD_DIM), lambda b, t: (b, 0, 0, 0)),
            pl.BlockSpec((None, POOL_HIST, b_width), lambda b, t: (b, 0, 0)),
        ]
        out_shape = [
            jax.ShapeDtypeStruct((bsz, t_len, d), F32),
            jax.ShapeDtypeStruct((bsz, n_heads, HEAD_DIM, HEAD_DIM), F32),
            jax.ShapeDtypeStruct((bsz, POOL_HIST, b_width), F32),
        ]
        scratch = [pltpu.VMEM((n_state, HEAD_DIM, HEAD_DIM), F32), pltpu.VMEM((POOL_HIST, b_width), F32)]
        semantics = ("arbitrary", "arbitrary")
    else:
        tile = bsz * t_len
        chunk = seq_rows = t_len
        assert t_len % 16 == 0 and t_len >= POOL_HIST
        n_seq, n_state = bsz, bsz * n_heads
        grid = (1,)
        x_in = x.reshape(tile, d)
        in_specs = ([pl.BlockSpec((tile, d), lambda i: (0, 0))] + w_specs
                    + [_resident(state.shape), _resident(hist.shape)])
        args = [x_in] + weights + [state, hist]
        out_specs = [
            pl.BlockSpec((tile, d), lambda i: (0, 0)),
            pl.BlockSpec(state.shape, lambda i: (0, 0, 0, 0)),
            pl.BlockSpec((bsz, POOL_HIST, b_width), lambda i: (0, 0, 0)),
        ]
        out_shape = [
            jax.ShapeDtypeStruct((tile, d), F32),
            jax.ShapeDtypeStruct(state.shape, F32),
            jax.ShapeDtypeStruct((bsz, POOL_HIST, b_width), F32),
        ]
        scratch = [pltpu.VMEM((n_state, HEAD_DIM, HEAD_DIM), F32)]
        semantics = ("arbitrary",)
    scratch += [
        pltpu.VMEM((n_seq * (POOL_HIST + seq_rows), b_width), F32),
        pltpu.VMEM((tile, n_heads * HEAD_DIM), BF16),
        pltpu.VMEM((tile, b_width), BF16),
    ]
    y, s_new, pool_new = pl.pallas_call(
        functools.partial(_mixer_kernel, layer=layer, n_heads=n_heads, tile=tile, chunk=chunk,
                          seq_rows=seq_rows, carry=carry, start_pos=start_pos,
                          g_pre=g_pre, g_post=g_post),
        grid=grid,
        in_specs=in_specs,
        out_specs=out_specs,
        out_shape=out_shape,
        scratch_shapes=scratch,
        compiler_params=pltpu.CompilerParams(
            dimension_semantics=semantics, vmem_limit_bytes=V7X_VMEM_LIMIT_BYTES),
        name="mixer_stream" if carry else "mixer_step",
    )(*args)
    return y.reshape(bsz, t_len, d), s_new, pool_new[:, 1:, :]


def kernel(x_prompt, x_sample, p_prompt, p_sample, state_hgrn, state_pool, norm_gains, lb_logits, w_ffn1_gu, w_ffn1_down, w_in, a_gnorm, w_a_up, w_pool, pool_scale, w_b_up, w_out, w_ffn2_gu, w_ffn2_down, w_ple_proj, w_ple_gate):
    depth = w_in.shape[0]
    past_len = 2048
    bp, tp, d = x_prompt.shape
    bs, ts, _ = x_sample.shape
    xp, xs = x_prompt, x_sample
    hp_list, bp_list, hs_list, bs_list = [], [], [], []
    for l in range(depth):
        gains = norm_gains[l]
        bf = lambda w: w[l].astype(BF16)
        ffn1 = functools.partial(_ffn, gains=gains, w_gu=bf(w_ffn1_gu), w_down=bf(w_ffn1_down), pre=0, post=1)
        ffn2 = functools.partial(_ffn, gains=gains, w_gu=bf(w_ffn2_gu), w_down=bf(w_ffn2_down), pre=4, post=5,
                                 w_pp=bf(w_ple_proj), w_pg=bf(w_ple_gate), ple=(6, 7))
        mixer = functools.partial(
            _mixer, gains=gains, lb_logits=lb_logits, w_in=bf(w_in), a_gnorm=a_gnorm[l].reshape(1, -1),
            w_a_up=bf(w_a_up), w_pool=bf(w_pool), pool_scale=pool_scale[l].reshape(1, -1),
            w_b_up=bf(w_b_up), w_out=bf(w_out), layer=l, g_pre=2, g_post=3)

        xp = ffn1(xp.reshape(bp * tp, d)).reshape(bp, tp, d)
        xs = ffn1(xs.reshape(bs * ts, d)).reshape(bs, ts, d)
        xp, hp, pp = mixer(xp, start_pos=0)
        xs, hs, ps = mixer(xs, state=state_hgrn[l], hist=state_pool[l], start_pos=past_len)
        xp = ffn2(xp.reshape(bp * tp, d), p=p_prompt[l].reshape(bp * tp, -1)).reshape(bp, tp, d)
        xs = ffn2(xs.reshape(bs * ts, d), p=p_sample[l].reshape(bs * ts, -1)).reshape(bs, ts, d)
        hp_list.append(hp)
        bp_list.append(pp)
        hs_list.append(hs)
        bs_list.append(ps)
    return (xp, xs, jnp.stack(hp_list), jnp.stack(bp_list), jnp.stack(hs_list), jnp.stack(bs_list))
```

```python
import functools

import jax
import jax.numpy as jnp
from jax import lax
from jax.experimental import pallas as pl
from jax.experimental.pallas import tpu as pltpu

EPS = 1e-6
HEAD_DIM = 128
POOL_WINDOWS = (2, 4, 8, 16)
POOL_GDIM = 128
POOL_HIST = 16
PROMPT_CHUNK = 64
V7X_VMEM_LIMIT_BYTES = 56 * 1024 * 1024

BF16 = jnp.bfloat16
F32 = jnp.float32


def _dot(a, b):
    return jnp.dot(a, b, preferred_element_type=F32)


def _dot_nt(a, b):
    return lax.dot_general(a, b, (((1,), (1,)), ((), ())), preferred_element_type=F32)


def _dot_tn(a, b):
    return lax.dot_general(a, b, (((0,), (0,)), ((), ())), preferred_element_type=F32)


def _rmsnorm(x, g):
    ms = jnp.mean(x * x, axis=-1, keepdims=True)
    return x * lax.rsqrt(ms + EPS) * g


def _silu(x):
    return x * jax.nn.sigmoid(x)


def _resident(shape):
    zeros = (0,) * len(shape)
    return pl.BlockSpec(shape, lambda *_: zeros, pipeline_mode=pl.Buffered(1))


def _ffn_kernel(*refs, d_ff, ff_chunk, pre, post, ple):
    if ple is None:
        x_ref, gains_ref, wgu_ref, wd_ref, o_ref, act_ref = refs
    else:
        x_ref, p_ref, gains_ref, wgu_ref, wd_ref, wpp_ref, wpg_ref, o_ref, act_ref = refs
    x = x_ref[...]
    h = _rmsnorm(x, gains_ref[pre:pre + 1, :]).astype(BF16)
    for c in range(d_ff // ff_chunk):
        lo = c * ff_chunk
        a = _dot(h, wgu_ref[:, lo:lo + ff_chunk])
        b = _dot(h, wgu_ref[:, d_ff + lo:d_ff + lo + ff_chunk])
        act_ref[:, lo:lo + ff_chunk] = (_silu(a) * b).astype(BF16)
    y = _dot(act_ref[...], wd_ref[...])
    x = x + 0.5 * _rmsnorm(y, gains_ref[post:post + 1, :])
    if ple is not None:
        g_pre, g_post = ple
        e = _dot(p_ref[...].astype(BF16), wpp_ref[...])
        gate = jax.nn.sigmoid(_dot(_rmsnorm(x, gains_ref[g_pre:g_pre + 1, :]).astype(BF16), wpg_ref[...]))
        x = x + _rmsnorm(gate * e, gains_ref[g_post:g_post + 1, :])
    o_ref[...] = x


def _ffn(x, gains, w_gu, w_down, *, pre, post, p=None, w_pp=None, w_pg=None, ple=None):
    n, d = x.shape
    d_ff = w_down.shape[0]
    tm = min(512, n)
    assert n % tm == 0
    ff_chunk = 256
    assert d_ff % ff_chunk == 0
    row = lambda width: pl.BlockSpec((tm, width), lambda i: (i, 0))
    in_specs = [row(d)]
    args = [x]
    if ple is not None:
        in_specs.append(row(p.shape[1]))
        args.append(p)
    in_specs += [_resident(gains.shape), _resident(w_gu.shape), _resident(w_down.shape)]
    args += [gains, w_gu, w_down]
    if ple is not None:
        in_specs += [_resident(w_pp.shape), _resident(w_pg.shape)]
        args += [w_pp, w_pg]
    return pl.pallas_call(
        functools.partial(_ffn_kernel, d_ff=d_ff, ff_chunk=ff_chunk, pre=pre, post=post, ple=ple),
        grid=(n // tm,),
        in_specs=in_specs,
        out_specs=row(d),
        out_shape=jax.ShapeDtypeStruct((n, d), F32),
        scratch_shapes=[pltpu.VMEM((tm, d_ff), BF16)],
        compiler_params=pltpu.CompilerParams(
            dimension_semantics=("arbitrary",), vmem_limit_bytes=V7X_VMEM_LIMIT_BYTES),
        name="ffn_ple" if ple is not None else "ffn",
    )(*args)


def _mixer_kernel(*refs, layer, n_heads, tile, chunk, seq_rows, carry, start_pos, g_pre, g_post):
    if carry:
        (x_ref, gains_ref, lbl_ref, win_ref, agn_ref, waup_ref, wpool_ref, pscale_ref, wbup_ref,
         wout_ref, o_ref, s_out_ref, pool_out_ref,
         st_scr, hist_scr, ext_scr, gated_scr, mixed_scr) = refs
    else:
        (x_ref, gains_ref, lbl_ref, win_ref, agn_ref, waup_ref, wpool_ref, pscale_ref, wbup_ref,
         wout_ref, s0_ref, hist0_ref, o_ref, s_out_ref, pool_out_ref,
         st_scr, ext_scr, gated_scr, mixed_scr) = refs
    a_width = n_heads * HEAD_DIM
    b_width = len(POOL_WINDOWS) * POOL_GDIM
    d_model = x_ref.shape[-1]
    n_seq = tile // seq_rows
    n_chunks = tile // chunk
    ext_rows = POOL_HIST + seq_rows
    step = pl.program_id(1) if carry else 0

    if carry:
        @pl.when(step == 0)
        def _():
            st_scr[...] = jnp.zeros_like(st_scr)
            hist_scr[...] = jnp.zeros_like(hist_scr)
    else:
        for s in range(n_seq):
            for hd in range(n_heads):
                st_scr[s * n_heads + hd] = s0_ref[s, hd].T

    x = x_ref[...]
    h = _rmsnorm(x, gains_ref[g_pre:g_pre + 1, :]).astype(BF16)

    logits = lbl_ref[...]
    ex = jnp.exp(logits - jnp.max(logits, axis=0, keepdims=True))
    lb = jnp.sum(ex[:layer + 1], axis=0, keepdims=True) / jnp.sum(ex, axis=0, keepdims=True)

    r_idx = lax.broadcasted_iota(jnp.int32, (tile, tile), 0)
    c_idx = lax.broadcasted_iota(jnp.int32, (tile, tile), 1)
    causal = (c_idx <= r_idx) & (c_idx >= (r_idx // chunk) * chunk)
    tri = jnp.where(causal, 1.0, 0.0).astype(BF16)

    heads = [slice(hd * HEAD_DIM, (hd + 1) * HEAD_DIM) for hd in range(n_heads)]
    blocks = [slice(c * chunk, (c + 1) * chunk) for c in range(n_chunks)]
    q = _dot(h, win_ref[:, 0:a_width])
    fz = _dot(h, win_ref[:, a_width:2 * a_width])
    v = _dot(h, win_ref[:, 2 * a_width:3 * a_width]).astype(BF16)
    e = jnp.exp(-jnp.abs(fz))
    r = 1.0 / (1.0 + e)
    pos = fz >= 0.0
    sig_p = jnp.where(pos, r, e * r)
    sig_n = jnp.where(pos, e * r, r)
    logf = jnp.log(lb + (1.0 - lb) * sig_p)
    kk = (1.0 - lb) * sig_n
    hi = logf.astype(BF16)
    lo = (logf - hi.astype(F32)).astype(BF16)
    g_cum = _dot(tri, hi) + _dot(tri, lo)
    dec = jnp.exp(g_cum)
    q_dec = (_silu(q) * dec).astype(BF16)
    k_inv = kk * jnp.exp(-g_cum)
    k_inv_bf = k_inv.astype(BF16)
    att = [jnp.where(causal, _dot_nt(q_dec[:, ls], k_inv_bf[:, ls]), 0.0).astype(BF16) for ls in heads]
    o_intra = [_dot(att[hd], v[:, heads[hd]]) for hd in range(n_heads)]
    d_last = [dec[rows.stop - 1:rows.stop, :] for rows in blocks]
    kv = [[_dot_tn(v[rows, ls], (k_inv[rows, ls] * d_last[c][:, ls]).astype(BF16)) for ls in heads]
          for c, rows in enumerate(blocks)]
    o_heads = [[] for _ in range(n_heads)]
    state = {}
    for c, rows in enumerate(blocks):
        for hd, ls in enumerate(heads):
            s_idx = hd if carry else (rows.start // seq_rows) * n_heads + hd
            st = state[s_idx] if s_idx in state else st_scr[s_idx]
            o_heads[hd].append(o_intra[hd][rows] + _dot_nt(q_dec[rows, ls], st.astype(BF16)))
            state[s_idx] = st * d_last[c][:, ls] + kv[c][hd]
    for s_idx, st in state.items():
        st_scr[s_idx] = st
    og = _dot(h, win_ref[:, 3 * a_width:4 * a_width])
    gated_og = _silu(og)
    for hd, ls in enumerate(heads):
        o = jnp.concatenate(o_heads[hd], axis=0) if n_chunks > 1 else o_heads[hd][0]
        o = o * lax.rsqrt(jnp.mean(o * o, axis=-1, keepdims=True) + EPS) * agn_ref[...]
        gated_scr[:, ls] = (o * gated_og[:, ls]).astype(BF16)
    y_a = _dot(gated_scr[...], waup_ref[...])

    u = _dot(h, win_ref[:, 4 * a_width:4 * a_width + b_width])
    row = lax.broadcasted_iota(jnp.int32, (seq_rows, 1), 0)
    first_pos = start_pos + (step * tile if carry else 0)
    for s in range(n_seq):
        base = s * ext_rows
        if carry:
            ext_scr[base:base + POOL_HIST, :] = hist_scr[...]
        else:
            ext_scr[base + 1:base + POOL_HIST, :] = hist0_ref[s]
        ext_scr[base + POOL_HIST:base + ext_rows, :] = u[s * seq_rows:(s + 1) * seq_rows]
    for s in range(n_seq):
        base = s * ext_rows
        for g, w in enumerate(POOL_WINDOWS):
            cols = slice(g * POOL_GDIM, (g + 1) * POOL_GDIM)
            cur = ext_scr[base + POOL_HIST:base + ext_rows, cols]
            acc = cur
            for j in range(1, w):
                acc = acc + ext_scr[base + POOL_HIST - j:base + ext_rows - j, cols]
            cnt = jnp.minimum(w, first_pos + row + 1).astype(F32)
            pooled = acc / cnt - cur
            mixed = _dot(pooled.astype(BF16), wpool_ref[g]) * pscale_ref[:, cols]
            mixed_scr[s * seq_rows:(s + 1) * seq_rows, cols] = mixed.astype(BF16)
        last = ext_scr[base + seq_rows:base + ext_rows, :]
        if carry:
            hist_scr[...] = last
            pool_out_ref[...] = last
        else:
            pool_out_ref[s] = last
    y_b = _dot(mixed_scr[...], wbup_ref[...])

    ga = _dot(h, win_ref[:, 4 * a_width + b_width:4 * a_width + b_width + d_model])
    gb = _dot(h, win_ref[:, 4 * a_width + b_width + d_model:4 * a_width + b_width + 2 * d_model])
    m = (jax.nn.sigmoid(ga) * y_a + jax.nn.sigmoid(gb) * y_b).astype(BF16)
    o_ref[...] = x + _rmsnorm(_dot(m, wout_ref[...]), gains_ref[g_post:g_post + 1, :])

    if carry:
        @pl.when(step == pl.num_programs(1) - 1)
        def _():
            for hd in range(n_heads):
                s_out_ref[hd] = st_scr[hd].T
    else:
        for s in range(n_seq):
            for hd in range(n_heads):
                s_out_ref[s, hd] = st_scr[s * n_heads + hd].T


def _mixer(x, gains, lb_logits, w_in, a_gnorm, w_a_up, w_pool, pool_scale, w_b_up, w_out,
           *, layer, state=None, hist=None, start_pos, g_pre, g_post):
    bsz, t_len, d = x.shape
    n_heads = w_a_up.shape[0] // HEAD_DIM
    b_width = w_b_up.shape[0]
    carry = state is None
    weights = [gains, lb_logits, w_in, a_gnorm, w_a_up, w_pool, pool_scale, w_b_up, w_out]
    w_specs = [_resident(w.shape) for w in weights]
    if carry:
        tile = min(256, t_len)
        chunk = min(PROMPT_CHUNK, tile)
        assert t_len % tile == 0 and tile % chunk == 0 and tile >= POOL_HIST
        seq_rows, n_seq, n_state = tile, 1, n_heads
        grid = (bsz, t_len // tile)
        x_in = x
        in_specs = [pl.BlockSpec((None, tile, d), lambda b, t: (b, t, 0))] + w_specs
        args = [x_in] + weights
        out_specs = [
            pl.BlockSpec((None, tile, d), lambda b, t: (b, t, 0)),
            pl.BlockSpec((None, n_heads, HEAD_DIM, HEAD_DIM), lambda b, t: (b, 0, 0, 0)),
            pl.BlockSpec((None, POOL_HIST, b_width), lambda b, t: (b, 0, 0)),
        ]
        out_shape = [
            jax.ShapeDtypeStruct((bsz, t_len, d), F32),
            jax.ShapeDtypeStruct((bsz, n_heads, HEAD_DIM, HEAD_DIM), F32),
            jax.ShapeDtypeStruct((bsz, POOL_HIST, b_width), F32),
        ]
        scratch = [pltpu.VMEM((n_state, HEAD_DIM, HEAD_DIM), F32), pltpu.VMEM((POOL_HIST, b_width), F32)]
        semantics = ("arbitrary", "arbitrary")
    else:
        tile = bsz * t_len
        chunk = seq_rows = t_len
        assert t_len % 16 == 0 and t_len >= POOL_HIST
        n_seq, n_state = bsz, bsz * n_heads
        grid = (1,)
        x_in = x.reshape(tile, d)
        in_specs = ([pl.BlockSpec((tile, d), lambda i: (0, 0))] + w_specs
                    + [_resident(state.shape), _resident(hist.shape)])
        args = [x_in] + weights + [state, hist]
        out_specs = [
            pl.BlockSpec((tile, d), lambda i: (0, 0)),
            pl.BlockSpec(state.shape, lambda i: (0, 0, 0, 0)),
            pl.BlockSpec((bsz, POOL_HIST, b_width), lambda i: (0, 0, 0)),
        ]
        out_shape = [
            jax.ShapeDtypeStruct((tile, d), F32),
            jax.ShapeDtypeStruct(state.shape, F32),
            jax.ShapeDtypeStruct((bsz, POOL_HIST, b_width), F32),
        ]
        scratch = [pltpu.VMEM((n_state, HEAD_DIM, HEAD_DIM), F32)]
        semantics = ("arbitrary",)
    scratch += [
        pltpu.VMEM((n_seq * (POOL_HIST + seq_rows), b_width), F32),
        pltpu.VMEM((tile, n_heads * HEAD_DIM), BF16),
        pltpu.VMEM((tile, b_width), BF16),
    ]
    y, s_new, pool_new = pl.pallas_call(
        functools.partial(_mixer_kernel, layer=layer, n_heads=n_heads, tile=tile, chunk=chunk,
                          seq_rows=seq_rows, carry=carry, start_pos=start_pos,
                          g_pre=g_pre, g_post=g_post),
        grid=grid,
        in_specs=in_specs,
        out_specs=out_specs,
        out_shape=out_shape,
        scratch_shapes=scratch,
        compiler_params=pltpu.CompilerParams(
            dimension_semantics=semantics, vmem_limit_bytes=V7X_VMEM_LIMIT_BYTES),
        name="mixer_stream" if carry else "mixer_step",
    )(*args)
    return y.reshape(bsz, t_len, d), s_new, pool_new[:, 1:, :]


def kernel(x_prompt, x_sample, p_prompt, p_sample, state_hgrn, state_pool, norm_gains, lb_logits, w_ffn1_gu, w_ffn1_down, w_in, a_gnorm, w_a_up, w_pool, pool_scale, w_b_up, w_out, w_ffn2_gu, w_ffn2_down, w_ple_proj, w_ple_gate):
    depth = w_in.shape[0]
    past_len = 2048
    bp, tp, d = x_prompt.shape
    bs, ts, _ = x_sample.shape
    xp, xs = x_prompt, x_sample
    hp_list, bp_list, hs_list, bs_list = [], [], [], []
    for l in range(depth):
        gains = norm_gains[l]
        bf = lambda w: w[l].astype(BF16)
        ffn1 = functools.partial(_ffn, gains=gains, w_gu=bf(w_ffn1_gu), w_down=bf(w_ffn1_down), pre=0, post=1)
        ffn2 = functools.partial(_ffn, gains=gains, w_gu=bf(w_ffn2_gu), w_down=bf(w_ffn2_down), pre=4, post=5,
                                 w_pp=bf(w_ple_proj), w_pg=bf(w_ple_gate), ple=(6, 7))
        mixer = functools.partial(
            _mixer, gains=gains, lb_logits=lb_logits, w_in=bf(w_in), a_gnorm=a_gnorm[l].reshape(1, -1),
            w_a_up=bf(w_a_up), w_pool=bf(w_pool), pool_scale=pool_scale[l].reshape(1, -1),
            w_b_up=bf(w_b_up), w_out=bf(w_out), layer=l, g_pre=2, g_post=3)

        xp = ffn1(xp.reshape(bp * tp, d)).reshape(bp, tp, d)
        xs = ffn1(xs.reshape(bs * ts, d)).reshape(bs, ts, d)
        xp, hp, pp = mixer(xp, start_pos=0)
        xs, hs, ps = mixer(xs, state=state_hgrn[l], hist=state_pool[l], start_pos=past_len)
        xp = ffn2(xp.reshape(bp * tp, d), p=p_prompt[l].reshape(bp * tp, -1)).reshape(bp, tp, d)
        xs = ffn2(xs.reshape(bs * ts, d), p=p_sample[l].reshape(bs * ts, -1)).reshape(bs, ts, d)
        hp_list.append(hp)
        bp_list.append(pp)
        hs_list.append(hs)
        bs_list.append(ps)
    return (xp, xs, jnp.stack(hp_list), jnp.stack(bp_list), jnp.stack(hs_list), jnp.stack(bs_list))
```

```python
import functools

import jax
import jax.numpy as jnp
from jax import lax
from jax.experimental import pallas as pl
from jax.experimental.pallas import tpu as pltpu

EPS = 1e-6
HEAD_DIM = 128
POOL_WINDOWS = (2, 4, 8, 16)
POOL_GDIM = 128
POOL_HIST = 16
PROMPT_CHUNK = 64
ATT_SPAN = 256
STREAM_TILE = 512
V7X_VMEM_LIMIT_BYTES = 56 * 1024 * 1024

BF16 = jnp.bfloat16
F32 = jnp.float32


def _dot(a, b):
    return jnp.dot(a, b, preferred_element_type=F32)


def _dot_nt(a, b):
    return lax.dot_general(a, b, (((1,), (1,)), ((), ())), preferred_element_type=F32)


def _dot_tn(a, b):
    return lax.dot_general(a, b, (((0,), (0,)), ((), ())), preferred_element_type=F32)


def _rmsnorm(x, g):
    ms = jnp.mean(x * x, axis=-1, keepdims=True)
    return x * lax.rsqrt(ms + EPS) * g


def _silu(x):
    return x * jax.nn.sigmoid(x)


def _resident(shape):
    zeros = (0,) * len(shape)
    return pl.BlockSpec(shape, lambda *_: zeros, pipeline_mode=pl.Buffered(1))


def _ffn_kernel(*refs, d_ff, ff_chunk, pre, post, ple):
    if ple is None:
        x_ref, gains_ref, wgu_ref, wd_ref, o_ref, act_ref = refs
    else:
        x_ref, p_ref, gains_ref, wgu_ref, wd_ref, wpp_ref, wpg_ref, o_ref, act_ref = refs
    x = x_ref[...]
    h = _rmsnorm(x, gains_ref[pre:pre + 1, :]).astype(BF16)
    for c in range(d_ff // ff_chunk):
        lo = c * ff_chunk
        a = _dot(h, wgu_ref[:, lo:lo + ff_chunk])
        b = _dot(h, wgu_ref[:, d_ff + lo:d_ff + lo + ff_chunk])
        act_ref[:, lo:lo + ff_chunk] = (_silu(a) * b).astype(BF16)
    y = _dot(act_ref[...], wd_ref[...])
    x = x + 0.5 * _rmsnorm(y, gains_ref[post:post + 1, :])
    if ple is not None:
        g_pre, g_post = ple
        e = _dot(p_ref[...].astype(BF16), wpp_ref[...])
        gate = jax.nn.sigmoid(_dot(_rmsnorm(x, gains_ref[g_pre:g_pre + 1, :]).astype(BF16), wpg_ref[...]))
        x = x + _rmsnorm(gate * e, gains_ref[g_post:g_post + 1, :])
    o_ref[...] = x


def _ffn(x, gains, w_gu, w_down, *, pre, post, p=None, w_pp=None, w_pg=None, ple=None):
    n, d = x.shape
    d_ff = w_down.shape[0]
    tm = min(1024, n)
    assert n % tm == 0
    ff_chunk = 256
    assert d_ff % ff_chunk == 0
    row = lambda width: pl.BlockSpec((tm, width), lambda i: (i, 0))
    in_specs = [row(d)]
    args = [x]
    if ple is not None:
        in_specs.append(row(p.shape[1]))
        args.append(p)
    in_specs += [_resident(gains.shape), _resident(w_gu.shape), _resident(w_down.shape)]
    args += [gains, w_gu, w_down]
    if ple is not None:
        in_specs += [_resident(w_pp.shape), _resident(w_pg.shape)]
        args += [w_pp, w_pg]
    return pl.pallas_call(
        functools.partial(_ffn_kernel, d_ff=d_ff, ff_chunk=ff_chunk, pre=pre, post=post, ple=ple),
        grid=(n // tm,),
        in_specs=in_specs,
        out_specs=row(d),
        out_shape=jax.ShapeDtypeStruct((n, d), F32),
        scratch_shapes=[pltpu.VMEM((tm, d_ff), BF16)],
        compiler_params=pltpu.CompilerParams(
            dimension_semantics=("arbitrary",), vmem_limit_bytes=V7X_VMEM_LIMIT_BYTES),
        name="ffn_ple" if ple is not None else "ffn",
    )(*args)


def _mixer_kernel(*refs, layer, n_heads, tile, span, chunk, seq_rows, carry, start_pos, g_pre, g_post):
    if carry:
        (x_ref, gains_ref, lbl_ref, win_ref, agn_ref, waup_ref, wpool_ref, pscale_ref, wbup_ref,
         wout_ref, o_ref, s_out_ref, pool_out_ref,
         st_scr, hist_scr, ext_scr, gated_scr, mixed_scr) = refs
    else:
        (x_ref, gains_ref, lbl_ref, win_ref, agn_ref, waup_ref, wpool_ref, pscale_ref, wbup_ref,
         wout_ref, s0_ref, hist0_ref, o_ref, s_out_ref, pool_out_ref,
         st_scr, ext_scr, gated_scr, mixed_scr) = refs
    a_width = n_heads * HEAD_DIM
    b_width = len(POOL_WINDOWS) * POOL_GDIM
    d_model = x_ref.shape[-1]
    n_seq = tile // seq_rows
    n_chunks = tile // chunk
    ext_rows = POOL_HIST + seq_rows
    step = pl.program_id(1) if carry else 0

    if carry:
        @pl.when(step == 0)
        def _():
            st_scr[...] = jnp.zeros_like(st_scr)
            hist_scr[...] = jnp.zeros_like(hist_scr)
    else:
        for s in range(n_seq):
            for hd in range(n_heads):
                st_scr[s * n_heads + hd] = s0_ref[s, hd].T

    x = x_ref[...]
    h = _rmsnorm(x, gains_ref[g_pre:g_pre + 1, :]).astype(BF16)

    logits = lbl_ref[...]
    ex = jnp.exp(logits - jnp.max(logits, axis=0, keepdims=True))
    lb = jnp.sum(ex[:layer + 1], axis=0, keepdims=True) / jnp.sum(ex, axis=0, keepdims=True)

    r_idx = lax.broadcasted_iota(jnp.int32, (span, span), 0)
    c_idx = lax.broadcasted_iota(jnp.int32, (span, span), 1)
    causal = (c_idx <= r_idx) & (c_idx >= (r_idx // chunk) * chunk)
    tri = jnp.where(causal, 1.0, 0.0).astype(BF16)

    heads = [slice(hd * HEAD_DIM, (hd + 1) * HEAD_DIM) for hd in range(n_heads)]
    blocks = [slice(c * chunk, (c + 1) * chunk) for c in range(n_chunks)]
    q = _dot(h, win_ref[:, 0:a_width])
    fz = _dot(h, win_ref[:, a_width:2 * a_width])
    v = _dot(h, win_ref[:, 2 * a_width:3 * a_width]).astype(BF16)
    e = jnp.exp(-jnp.abs(fz))
    r = 1.0 / (1.0 + e)
    pos = fz >= 0.0
    sig_p = jnp.where(pos, r, e * r)
    sig_n = jnp.where(pos, e * r, r)
    logf = jnp.log(lb + (1.0 - lb) * sig_p)
    kk = (1.0 - lb) * sig_n
    hi = logf.astype(BF16)
    lo = (logf - hi.astype(F32)).astype(BF16)
    spans = [slice(s * span, (s + 1) * span) for s in range(tile // span)]
    g_parts = [_dot(tri, hi[sp]) + _dot(tri, lo[sp]) for sp in spans]
    g_cum = jnp.concatenate(g_parts, axis=0) if len(spans) > 1 else g_parts[0]
    dec = jnp.exp(g_cum)
    q_dec = (_silu(q) * dec).astype(BF16)
    k_inv = kk * jnp.exp(-g_cum)
    k_inv_bf = k_inv.astype(BF16)
    att = [[jnp.where(causal, _dot_nt(q_dec[sp, ls], k_inv_bf[sp, ls]), 0.0).astype(BF16) for ls in heads]
           for sp in spans]
    o_intra = [[_dot(att[s][hd], v[sp, ls]) for hd, ls in enumerate(heads)] for s, sp in enumerate(spans)]
    d_last = [dec[rows.stop - 1:rows.stop, :] for rows in blocks]
    kv = [[_dot_tn(v[rows, ls], (k_inv[rows, ls] * d_last[c][:, ls]).astype(BF16)) for ls in heads]
          for c, rows in enumerate(blocks)]
    o_heads = [[] for _ in range(n_heads)]
    state = {}
    for c, rows in enumerate(blocks):
        for hd, ls in enumerate(heads):
            s_idx = hd if carry else (rows.start // seq_rows) * n_heads + hd
            st = state[s_idx] if s_idx in state else st_scr[s_idx]
            s, lo_row = divmod(rows.start, span)
            o_heads[hd].append(o_intra[s][hd][lo_row:lo_row + chunk]
                               + _dot_nt(q_dec[rows, ls], st.astype(BF16)))
            state[s_idx] = st * d_last[c][:, ls] + kv[c][hd]
    for s_idx, st in state.items():
        st_scr[s_idx] = st
    og = _dot(h, win_ref[:, 3 * a_width:4 * a_width])
    gated_og = _silu(og)
    for hd, ls in enumerate(heads):
        o = jnp.concatenate(o_heads[hd], axis=0) if n_chunks > 1 else o_heads[hd][0]
        o = o * lax.rsqrt(jnp.mean(o * o, axis=-1, keepdims=True) + EPS) * agn_ref[...]
        gated_scr[:, ls] = (o * gated_og[:, ls]).astype(BF16)
    y_a = _dot(gated_scr[...], waup_ref[...])

    u = _dot(h, win_ref[:, 4 * a_width:4 * a_width + b_width])
    row = lax.broadcasted_iota(jnp.int32, (seq_rows, 1), 0)
    first_pos = start_pos + (step * tile if carry else 0)
    for s in range(n_seq):
        base = s * ext_rows
        if carry:
            ext_scr[base:base + POOL_HIST, :] = hist_scr[...]
        else:
            ext_scr[base + 1:base + POOL_HIST, :] = hist0_ref[s]
        ext_scr[base + POOL_HIST:base + ext_rows, :] = u[s * seq_rows:(s + 1) * seq_rows]
    for s in range(n_seq):
        base = s * ext_rows
        for g, w in enumerate(POOL_WINDOWS):
            cols = slice(g * POOL_GDIM, (g + 1) * POOL_GDIM)
            cur = ext_scr[base + POOL_HIST:base + ext_rows, cols]
            acc = cur
            for j in range(1, w):
                acc = acc + ext_scr[base + POOL_HIST - j:base + ext_rows - j, cols]
            cnt = jnp.minimum(w, first_pos + row + 1).astype(F32)
            pooled = acc / cnt - cur
            mixed = _dot(pooled.astype(BF16), wpool_ref[g]) * pscale_ref[:, cols]
            mixed_scr[s * seq_rows:(s + 1) * seq_rows, cols] = mixed.astype(BF16)
        last = ext_scr[base + seq_rows:base + ext_rows, :]
        if carry:
            hist_scr[...] = last
            pool_out_ref[...] = last
        else:
            pool_out_ref[s] = last
    y_b = _dot(mixed_scr[...], wbup_ref[...])

    ga = _dot(h, win_ref[:, 4 * a_width + b_width:4 * a_width + b_width + d_model])
    gb = _dot(h, win_ref[:, 4 * a_width + b_width + d_model:4 * a_width + b_width + 2 * d_model])
    m = (jax.nn.sigmoid(ga) * y_a + jax.nn.sigmoid(gb) * y_b).astype(BF16)
    o_ref[...] = x + _rmsnorm(_dot(m, wout_ref[...]), gains_ref[g_post:g_post + 1, :])

    if carry:
        @pl.when(step == pl.num_programs(1) - 1)
        def _():
            for hd in range(n_heads):
                s_out_ref[hd] = st_scr[hd].T
    else:
        for s in range(n_seq):
            for hd in range(n_heads):
                s_out_ref[s, hd] = st_scr[s * n_heads + hd].T


def _mixer(x, gains, lb_logits, w_in, a_gnorm, w_a_up, w_pool, pool_scale, w_b_up, w_out,
           *, layer, state=None, hist=None, start_pos, g_pre, g_post):
    bsz, t_len, d = x.shape
    n_heads = w_a_up.shape[0] // HEAD_DIM
    b_width = w_b_up.shape[0]
    carry = state is None
    weights = [gains, lb_logits, w_in, a_gnorm, w_a_up, w_pool, pool_scale, w_b_up, w_out]
    w_specs = [_resident(w.shape) for w in weights]
    if carry:
        tile = min(STREAM_TILE, t_len)
        chunk = min(PROMPT_CHUNK, tile)
        assert t_len % tile == 0 and tile % chunk == 0 and tile >= POOL_HIST
        seq_rows, n_seq, n_state = tile, 1, n_heads
        grid = (bsz, t_len // tile)
        x_in = x
        in_specs = [pl.BlockSpec((None, tile, d), lambda b, t: (b, t, 0))] + w_specs
        args = [x_in] + weights
        out_specs = [
            pl.BlockSpec((None, tile, d), lambda b, t: (b, t, 0)),
            pl.BlockSpec((None, n_heads, HEAD_DIM, HEAD_DIM), lambda b, t: (b, 0, 0, 0)),
            pl.BlockSpec((None, POOL_HIST, b_width), lambda b, t: (b, 0, 0)),
        ]
        out_shape = [
            jax.ShapeDtypeStruct((bsz, t_len, d), F32),
            jax.ShapeDtypeStruct((bsz, n_heads, HEAD_DIM, HEAD_DIM), F32),
            jax.ShapeDtypeStruct((bsz, POOL_HIST, b_width), F32),
        ]
        scratch = [pltpu.VMEM((n_state, HEAD_DIM, HEAD_DIM), F32), pltpu.VMEM((POOL_HIST, b_width), F32)]
        semantics = ("arbitrary", "arbitrary")
    else:
        tile = bsz * t_len
        chunk = seq_rows = t_len
        assert t_len % 16 == 0 and t_len >= POOL_HIST
        n_seq, n_state = bsz, bsz * n_heads
        grid = (1,)
        x_in = x.reshape(tile, d)
        in_specs = ([pl.BlockSpec((tile, d), lambda i: (0, 0))] + w_specs
                    + [_resident(state.shape), _resident(hist.shape)])
        args = [x_in] + weights + [state, hist]
        out_specs = [
            pl.BlockSpec((tile, d), lambda i: (0, 0)),
            pl.BlockSpec(state.shape, lambda i: (0, 0, 0, 0)),
            pl.BlockSpec((bsz, POOL_HIST, b_width), lambda i: (0, 0, 0)),
        ]
        out_shape = [
            jax.ShapeDtypeStruct((tile, d), F32),
            jax.ShapeDtypeStruct(state.shape, F32),
            jax.ShapeDtypeStruct((bsz, POOL_HIST, b_width), F32),
        ]
        scratch = [pltpu.VMEM((n_state, HEAD_DIM, HEAD_DIM), F32)]
        semantics = ("arbitrary",)
    scratch += [
        pltpu.VMEM((n_seq * (POOL_HIST + seq_rows), b_width), F32),
        pltpu.VMEM((tile, n_heads * HEAD_DIM), BF16),
        pltpu.VMEM((tile, b_width), BF16),
    ]
    span = min(ATT_SPAN, tile)
    assert tile % span == 0 and span % chunk == 0
    y, s_new, pool_new = pl.pallas_call(
        functools.partial(_mixer_kernel, layer=layer, n_heads=n_heads, tile=tile, span=span, chunk=chunk,
                          seq_rows=seq_rows, carry=carry, start_pos=start_pos,
                          g_pre=g_pre, g_post=g_post),
        grid=grid,
        in_specs=in_specs,
        out_specs=out_specs,
        out_shape=out_shape,
        scratch_shapes=scratch,
        compiler_params=pltpu.CompilerParams(
            dimension_semantics=semantics, vmem_limit_bytes=V7X_VMEM_LIMIT_BYTES),
        name="mixer_stream" if carry else "mixer_step",
    )(*args)
    return y.reshape(bsz, t_len, d), s_new, pool_new[:, 1:, :]


def kernel(x_prompt, x_sample, p_prompt, p_sample, state_hgrn, state_pool, norm_gains, lb_logits, w_ffn1_gu, w_ffn1_down, w_in, a_gnorm, w_a_up, w_pool, pool_scale, w_b_up, w_out, w_ffn2_gu, w_ffn2_down, w_ple_proj, w_ple_gate):
    depth = w_in.shape[0]
    past_len = 2048
    bp, tp, d = x_prompt.shape
    bs, ts, _ = x_sample.shape
    xp, xs = x_prompt, x_sample
    hp_list, bp_list, hs_list, bs_list = [], [], [], []
    for l in range(depth):
        gains = norm_gains[l]
        bf = lambda w: w[l].astype(BF16)
        ffn1 = functools.partial(_ffn, gains=gains, w_gu=bf(w_ffn1_gu), w_down=bf(w_ffn1_down), pre=0, post=1)
        ffn2 = functools.partial(_ffn, gains=gains, w_gu=bf(w_ffn2_gu), w_down=bf(w_ffn2_down), pre=4, post=5,
                                 w_pp=bf(w_ple_proj), w_pg=bf(w_ple_gate), ple=(6, 7))
        mixer = functools.partial(
            _mixer, gains=gains, lb_logits=lb_logits, w_in=bf(w_in), a_gnorm=a_gnorm[l].reshape(1, -1),
            w_a_up=bf(w_a_up), w_pool=bf(w_pool), pool_scale=pool_scale[l].reshape(1, -1),
            w_b_up=bf(w_b_up), w_out=bf(w_out), layer=l, g_pre=2, g_post=3)

        xp = ffn1(xp.reshape(bp * tp, d)).reshape(bp, tp, d)
        xs = ffn1(xs.reshape(bs * ts, d)).reshape(bs, ts, d)
        xp, hp, pp = mixer(xp, start_pos=0)
        xs, hs, ps = mixer(xs, state=state_hgrn[l], hist=state_pool[l], start_pos=past_len)
        xp = ffn2(xp.reshape(bp * tp, d), p=p_prompt[l].reshape(bp * tp, -1)).reshape(bp, tp, d)
        xs = ffn2(xs.reshape(bs * ts, d), p=p_sample[l].reshape(bs * ts, -1)).reshape(bs, ts, d)
        hp_list.append(hp)
        bp_list.append(pp)
        hs_list.append(hs)
        bs_list.append(ps)
    return (xp, xs, jnp.stack(hp_list), jnp.stack(bp_list), jnp.stack(hs_list), jnp.stack(bs_list))
```

```python
import functools

import jax
import jax.numpy as jnp
from jax import lax
from jax.experimental import pallas as pl
from jax.experimental.pallas import tpu as pltpu

EPS = 1e-6
HEAD_DIM = 128
POOL_WINDOWS = (2, 4, 8, 16)
POOL_GDIM = 128
POOL_HIST = 16
POOL_PAD = 8
DECAY_BLK = 64
ATT_SPAN = 256
STREAM_TILE = 512
V7X_VMEM_LIMIT_BYTES = 56 * 1024 * 1024

BF16 = jnp.bfloat16
F32 = jnp.float32


def _dot(a, b):
    return jnp.dot(a, b, preferred_element_type=F32)


def _dot_nt(a, b):
    return lax.dot_general(a, b, (((1,), (1,)), ((), ())), preferred_element_type=F32)


def _dot_tn(a, b):
    return lax.dot_general(a, b, (((0,), (0,)), ((), ())), preferred_element_type=F32)


def _rmsnorm(x, g):
    ms = jnp.mean(x * x, axis=-1, keepdims=True)
    return x * lax.rsqrt(ms + EPS) * g


def _silu(x):
    return x * jax.nn.sigmoid(x)


def _resident(shape):
    zeros = (0,) * len(shape)
    return pl.BlockSpec(shape, lambda *_: zeros, pipeline_mode=pl.Buffered(1))


def _ffn_kernel(*refs, d_ff, ff_chunk, pre, post, ple):
    if ple is None:
        x_ref, gains_ref, wgu_ref, wd_ref, o_ref, act_ref = refs
    else:
        x_ref, p_ref, gains_ref, wgu_ref, wd_ref, wpp_ref, wpg_ref, o_ref, act_ref = refs
    x = x_ref[...]
    h = _rmsnorm(x, gains_ref[pre:pre + 1, :]).astype(BF16)
    for c in range(d_ff // ff_chunk):
        lo = c * ff_chunk
        a = _dot(h, wgu_ref[:, lo:lo + ff_chunk])
        b = _dot(h, wgu_ref[:, d_ff + lo:d_ff + lo + ff_chunk])
        act_ref[:, lo:lo + ff_chunk] = (_silu(a) * b).astype(BF16)
    y = _dot(act_ref[...], wd_ref[...])
    x = x + 0.5 * _rmsnorm(y, gains_ref[post:post + 1, :])
    if ple is not None:
        g_pre, g_post = ple
        e = _dot(p_ref[...].astype(BF16), wpp_ref[...])
        gate = jax.nn.sigmoid(_dot(_rmsnorm(x, gains_ref[g_pre:g_pre + 1, :]).astype(BF16), wpg_ref[...]))
        x = x + _rmsnorm(gate * e, gains_ref[g_post:g_post + 1, :])
    o_ref[...] = x


def _ffn(x, gains, w_gu, w_down, *, pre, post, p=None, w_pp=None, w_pg=None, ple=None):
    n, d = x.shape
    d_ff = w_down.shape[0]
    tm = min(1024, n)
    assert n % tm == 0
    ff_chunk = 256
    assert d_ff % ff_chunk == 0
    row = lambda width: pl.BlockSpec((tm, width), lambda i: (i, 0))
    in_specs = [row(d)]
    args = [x]
    if ple is not None:
        in_specs.append(row(p.shape[1]))
        args.append(p)
    in_specs += [_resident(gains.shape), _resident(w_gu.shape), _resident(w_down.shape)]
    args += [gains, w_gu, w_down]
    if ple is not None:
        in_specs += [_resident(w_pp.shape), _resident(w_pg.shape)]
        args += [w_pp, w_pg]
    return pl.pallas_call(
        functools.partial(_ffn_kernel, d_ff=d_ff, ff_chunk=ff_chunk, pre=pre, post=post, ple=ple),
        grid=(n // tm,),
        in_specs=in_specs,
        out_specs=row(d),
        out_shape=jax.ShapeDtypeStruct((n, d), F32),
        scratch_shapes=[pltpu.VMEM((tm, d_ff), BF16)],
        compiler_params=pltpu.CompilerParams(
            dimension_semantics=("arbitrary",), vmem_limit_bytes=V7X_VMEM_LIMIT_BYTES),
        name="ffn_ple" if ple is not None else "ffn",
    )(*args)


def _mixer_kernel(*refs, layer, n_heads, tile, span, chunk, blk, seq_rows, carry, start_pos, g_pre, g_post):
    if carry:
        (x_ref, gains_ref, lbl_ref, win_ref, agn_ref, waup_ref, wpool_ref, pscale_ref, wbup_ref,
         wout_ref, o_ref, s_out_ref, pool_out_ref,
         st_scr, hist_scr, lvl_scr, gated_scr, mixed_scr) = refs
    else:
        (x_ref, gains_ref, lbl_ref, win_ref, agn_ref, waup_ref, wpool_ref, pscale_ref, wbup_ref,
         wout_ref, s0_ref, hist0_ref, o_ref, s_out_ref, pool_out_ref,
         st_scr, lvl_scr, gated_scr, mixed_scr) = refs
    a_width = n_heads * HEAD_DIM
    b_width = len(POOL_WINDOWS) * POOL_GDIM
    d_model = x_ref.shape[-1]
    n_seq = tile // seq_rows
    n_chunks = tile // chunk
    ext_rows = POOL_PAD + POOL_HIST + seq_rows
    step = pl.program_id(1) if carry else 0

    if carry:
        @pl.when(step == 0)
        def _():
            st_scr[...] = jnp.zeros_like(st_scr)
            hist_scr[...] = jnp.zeros_like(hist_scr)
    else:
        for s in range(n_seq):
            for hd in range(n_heads):
                st_scr[s * n_heads + hd] = s0_ref[s, hd].T

    x = x_ref[...]
    h = _rmsnorm(x, gains_ref[g_pre:g_pre + 1, :]).astype(BF16)

    logits = lbl_ref[...]
    ex = jnp.exp(logits - jnp.max(logits, axis=0, keepdims=True))
    lb = jnp.sum(ex[:layer + 1], axis=0, keepdims=True) / jnp.sum(ex, axis=0, keepdims=True)

    r_idx = lax.broadcasted_iota(jnp.int32, (span, span), 0)
    c_idx = lax.broadcasted_iota(jnp.int32, (span, span), 1)
    tri = jnp.where((c_idx <= r_idx) & (c_idx >= (r_idx // blk) * blk), 1.0, 0.0).astype(BF16)
    causal = (c_idx <= r_idx) & (c_idx >= (r_idx // chunk) * chunk)

    heads = [slice(hd * HEAD_DIM, (hd + 1) * HEAD_DIM) for hd in range(n_heads)]
    blocks = [slice(c * chunk, (c + 1) * chunk) for c in range(n_chunks)]
    subs = [slice(b * blk, (b + 1) * blk) for b in range(tile // blk)]
    two_level = chunk == 2 * blk
    assert two_level or chunk == blk
    proj = lambda lo_col, width: _dot(h, win_ref[:, lo_col:lo_col + width])
    fz = proj(a_width, a_width)
    q = proj(0, a_width)
    e = jnp.exp(-jnp.abs(fz))
    r = 1.0 / (1.0 + e)
    kk = (1.0 - lb) * jnp.where(fz >= 0.0, e * r, r)
    logf = jnp.log2(1.0 - kk)
    hi = logf.astype(BF16)
    lo = (logf - hi.astype(F32)).astype(BF16)
    v = proj(2 * a_width, a_width).astype(BF16)
    u = proj(4 * a_width, b_width)
    spans = [slice(s * span, (s + 1) * span) for s in range(tile // span)]
    g_parts = [_dot(tri, hi[sp]) + _dot(tri, lo[sp]) for sp in spans]
    g_cum = jnp.concatenate(g_parts, axis=0) if len(spans) > 1 else g_parts[0]
    og = proj(3 * a_width, a_width)
    ga = proj(4 * a_width + b_width, d_model)
    gb = proj(4 * a_width + b_width + d_model, d_model)
    rows_cat = lambda parts: jnp.concatenate(parts, axis=0) if len(parts) > 1 else parts[0]
    dec = jnp.exp2(g_cum)
    sub_dec = [dec[sb.stop - 1:sb.stop, :] for sb in subs]
    q_loc = _silu(q) * dec
    k_loc = kk * jnp.exp2(-g_cum)
    k_fwd = rows_cat([k_loc[sb] * sub_dec[b] for b, sb in enumerate(subs)])
    if two_level:
        q_chunk = rows_cat([q_loc[sb] * sub_dec[b - 1] if b % 2 else q_loc[sb] for b, sb in enumerate(subs)])
        k_end = rows_cat([k_fwd[sb] if b % 2 else k_fwd[sb] * sub_dec[b + 1] for b, sb in enumerate(subs)])
        d_chunk = [sub_dec[2 * c] * sub_dec[2 * c + 1] for c in range(n_chunks)]
    else:
        q_chunk, k_end, d_chunk = q_loc, k_fwd, sub_dec
    q_loc_bf, k_loc_bf = q_loc.astype(BF16), k_loc.astype(BF16)
    q_chunk_bf, k_end_bf = q_chunk.astype(BF16), k_end.astype(BF16)
    if two_level:
        k_fwd_bf = k_fwd.astype(BF16)
        zero = jnp.zeros((blk, HEAD_DIM), BF16)

        def scores(sp, ls):
            bs = range(sp.start // blk, sp.stop // blk)
            lhs = jnp.concatenate(
                [rows_cat([zero if b % 2 else q_loc_bf[subs[b], ls] for b in bs]),
                 rows_cat([q_loc_bf[subs[b], ls] if b % 2 else zero for b in bs])], axis=1)
            rhs = jnp.concatenate(
                [rows_cat([zero if b % 2 else k_loc_bf[subs[b], ls] for b in bs]),
                 rows_cat([k_loc_bf[subs[b], ls] if b % 2 else k_fwd_bf[subs[b], ls] for b in bs])], axis=1)
            return _dot_nt(lhs, rhs)
    else:
        scores = lambda sp, ls: _dot_nt(q_loc_bf[sp, ls], k_loc_bf[sp, ls])
    att = [[jnp.where(causal, scores(sp, ls), 0.0).astype(BF16) for ls in heads] for sp in spans]
    o_intra = [[_dot(att[s][hd], v[sp, ls]) for hd, ls in enumerate(heads)] for s, sp in enumerate(spans)]
    kv = [[_dot_tn(v[rows, ls], k_end_bf[rows, ls]) for ls in heads] for rows in blocks]
    o_heads = [[] for _ in range(n_heads)]
    state = {}
    for c, rows in enumerate(blocks):
        for hd, ls in enumerate(heads):
            s_idx = hd if carry else (rows.start // seq_rows) * n_heads + hd
            st = state[s_idx] if s_idx in state else st_scr[s_idx]
            s, lo_row = divmod(rows.start, span)
            o_heads[hd].append(o_intra[s][hd][lo_row:lo_row + chunk]
                               + _dot_nt(q_chunk_bf[rows, ls], st.astype(BF16)))
            state[s_idx] = st * d_chunk[c][:, ls] + kv[c][hd]
    for s_idx, st in state.items():
        st_scr[s_idx] = st
    gated_og = _silu(og)
    for hd, ls in enumerate(heads):
        o = jnp.concatenate(o_heads[hd], axis=0) if n_chunks > 1 else o_heads[hd][0]
        o = o * lax.rsqrt(jnp.mean(o * o, axis=-1, keepdims=True) + EPS) * agn_ref[...]
        gated_scr[:, ls] = (o * gated_og[:, ls]).astype(BF16)
    y_a = _dot(gated_scr[...], waup_ref[...])

    row = lax.broadcasted_iota(jnp.int32, (seq_rows, 1), 0)
    first_pos = start_pos + (step * tile if carry else 0)
    n_levels = len(POOL_WINDOWS)
    front = POOL_PAD + POOL_HIST
    for s in range(n_seq):
        base = s * ext_rows
        for k in range(n_levels - 1):
            lvl_scr[k, base:base + POOL_PAD, :] = jnp.zeros((POOL_PAD, b_width), F32)
        if carry:
            lvl_scr[0, base + POOL_PAD:base + front, :] = hist_scr[...]
        else:
            lvl_scr[0, base + POOL_PAD:base + POOL_PAD + 1, :] = jnp.zeros((1, b_width), F32)
            lvl_scr[0, base + POOL_PAD + 1:base + front, :] = hist0_ref[s]
        lvl_scr[0, base + front:base + ext_rows, :] = u[s * seq_rows:(s + 1) * seq_rows]
    for s in range(n_seq):
        base = s * ext_rows
        out_rows = slice(s * seq_rows, (s + 1) * seq_rows)
        for k in range(1, n_levels + 1):
            g, w, shift = k - 1, POOL_WINDOWS[k - 1], 2 ** (k - 1)
            assert w == 2 * shift
            lo_col = g * POOL_GDIM
            cols = slice(lo_col, lo_col + POOL_GDIM)
            lo_row = base + (front if k == n_levels else POOL_PAD)
            summed = (lvl_scr[k - 1, lo_row:base + ext_rows, lo_col:]
                      + lvl_scr[k - 1, lo_row - shift:base + ext_rows - shift, lo_col:])
            if k < n_levels:
                lvl_scr[k, lo_row:base + ext_rows, lo_col:] = summed
                summed = summed[POOL_HIST:, :POOL_GDIM]
            cur = lvl_scr[0, base + front:base + ext_rows, cols]
            cnt = jnp.minimum(w, first_pos + row + 1).astype(F32)
            pooled = summed / cnt - cur
            mixed = _dot(pooled.astype(BF16), wpool_ref[g]) * pscale_ref[:, cols]
            mixed_scr[out_rows, cols] = mixed.astype(BF16)
        last = lvl_scr[0, base + ext_rows - POOL_HIST:base + ext_rows, :]
        if carry:
            hist_scr[...] = last
            pool_out_ref[...] = last
        else:
            pool_out_ref[s] = last
    y_b = _dot(mixed_scr[...], wbup_ref[...])

    m = (jax.nn.sigmoid(ga) * y_a + jax.nn.sigmoid(gb) * y_b).astype(BF16)
    o_ref[...] = x + _rmsnorm(_dot(m, wout_ref[...]), gains_ref[g_post:g_post + 1, :])

    if carry:
        @pl.when(step == pl.num_programs(1) - 1)
        def _():
            for hd in range(n_heads):
                s_out_ref[hd] = st_scr[hd].T
    else:
        for s in range(n_seq):
            for hd in range(n_heads):
                s_out_ref[s, hd] = st_scr[s * n_heads + hd].T


def _mixer(x, gains, lb_logits, w_in, a_gnorm, w_a_up, w_pool, pool_scale, w_b_up, w_out,
           *, layer, state=None, hist=None, start_pos, g_pre, g_post):
    bsz, t_len, d = x.shape
    n_heads = w_a_up.shape[0] // HEAD_DIM
    b_width = w_b_up.shape[0]
    carry = state is None
    weights = [gains, lb_logits, w_in, a_gnorm, w_a_up, w_pool, pool_scale, w_b_up, w_out]
    w_specs = [_resident(w.shape) for w in weights]
    if carry:
        tile = min(STREAM_TILE, t_len)
        blk = min(DECAY_BLK, tile)
        chunk = min(2 * blk, tile)
        assert t_len % tile == 0 and tile % chunk == 0 and tile >= POOL_HIST
        seq_rows, n_seq, n_state = tile, 1, n_heads
        grid = (bsz, t_len // tile)
        x_in = x
        in_specs = [pl.BlockSpec((None, tile, d), lambda b, t: (b, t, 0))] + w_specs
        args = [x_in] + weights
        out_specs = [
            pl.BlockSpec((None, tile, d), lambda b, t: (b, t, 0)),
            pl.BlockSpec((None, n_heads, HEAD_DIM, HEAD_DIM), lambda b, t: (b, 0, 0, 0)),
            pl.BlockSpec((None, POOL_HIST, b_width), lambda b, t: (b, 0, 0)),
        ]
        out_shape = [
            jax.ShapeDtypeStruct((bsz, t_len, d), F32),
            jax.ShapeDtypeStruct((bsz, n_heads, HEAD_DIM, HEAD_DIM), F32),
            jax.ShapeDtypeStruct((bsz, POOL_HIST, b_width), F32),
        ]
        scratch = [pltpu.VMEM((n_state, HEAD_DIM, HEAD_DIM), F32), pltpu.VMEM((POOL_HIST, b_width), F32)]
        semantics = ("arbitrary", "arbitrary")
    else:
        tile = bsz * t_len
        chunk = blk = seq_rows = t_len
        assert t_len % 16 == 0 and t_len >= POOL_HIST
        n_seq, n_state = bsz, bsz * n_heads
        grid = (1,)
        x_in = x.reshape(tile, d)
        in_specs = ([pl.BlockSpec((tile, d), lambda i: (0, 0))] + w_specs
                    + [_resident(state.shape), _resident(hist.shape)])
        args = [x_in] + weights + [state, hist]
        out_specs = [
            pl.BlockSpec((tile, d), lambda i: (0, 0)),
            pl.BlockSpec(state.shape, lambda i: (0, 0, 0, 0)),
            pl.BlockSpec((bsz, POOL_HIST, b_width), lambda i: (0, 0, 0)),
        ]
        out_shape = [
            jax.ShapeDtypeStruct((tile, d), F32),
            jax.ShapeDtypeStruct(state.shape, F32),
            jax.ShapeDtypeStruct((bsz, POOL_HIST, b_width), F32),
        ]
        scratch = [pltpu.VMEM((n_state, HEAD_DIM, HEAD_DIM), F32)]
        semantics = ("arbitrary",)
    scratch += [
        pltpu.VMEM((len(POOL_WINDOWS), n_seq * (POOL_PAD + POOL_HIST + seq_rows), b_width), F32),
        pltpu.VMEM((tile, n_heads * HEAD_DIM), BF16),
        pltpu.VMEM((tile, b_width), BF16),
    ]
    span = min(ATT_SPAN, tile)
    assert tile % span == 0 and span % chunk == 0
    y, s_new, pool_new = pl.pallas_call(
        functools.partial(_mixer_kernel, layer=layer, n_heads=n_heads, tile=tile, span=span, chunk=chunk, blk=blk,
                          seq_rows=seq_rows, carry=carry, start_pos=start_pos,
                          g_pre=g_pre, g_post=g_post),
        grid=grid,
        in_specs=in_specs,
        out_specs=out_specs,
        out_shape=out_shape,
        scratch_shapes=scratch,
        compiler_params=pltpu.CompilerParams(
            dimension_semantics=semantics, vmem_limit_bytes=V7X_VMEM_LIMIT_BYTES),
        name="mixer_stream" if carry else "mixer_step",
    )(*args)
    return y.reshape(bsz, t_len, d), s_new, pool_new[:, 1:, :]


def kernel(x_prompt, x_sample, p_prompt, p_sample, state_hgrn, state_pool, norm_gains, lb_logits, w_ffn1_gu, w_ffn1_down, w_in, a_gnorm, w_a_up, w_pool, pool_scale, w_b_up, w_out, w_ffn2_gu, w_ffn2_down, w_ple_proj, w_ple_gate):
    depth = w_in.shape[0]
    past_len = 2048
    bp, tp, d = x_prompt.shape
    bs, ts, _ = x_sample.shape
    xp, xs = x_prompt, x_sample
    hp_list, bp_list, hs_list, bs_list = [], [], [], []
    for l in range(depth):
        gains = norm_gains[l]
        bf = lambda w: w[l].astype(BF16)
        ffn1 = functools.partial(_ffn, gains=gains, w_gu=bf(w_ffn1_gu), w_down=bf(w_ffn1_down), pre=0, post=1)
        ffn2 = functools.partial(_ffn, gains=gains, w_gu=bf(w_ffn2_gu), w_down=bf(w_ffn2_down), pre=4, post=5,
                                 w_pp=bf(w_ple_proj), w_pg=bf(w_ple_gate), ple=(6, 7))
        mixer = functools.partial(
            _mixer, gains=gains, lb_logits=lb_logits, w_in=bf(w_in), a_gnorm=a_gnorm[l].reshape(1, -1),
            w_a_up=bf(w_a_up), w_pool=bf(w_pool), pool_scale=pool_scale[l].reshape(1, -1),
            w_b_up=bf(w_b_up), w_out=bf(w_out), layer=l, g_pre=2, g_post=3)

        xp = ffn1(xp.reshape(bp * tp, d)).reshape(bp, tp, d)
        xs = ffn1(xs.reshape(bs * ts, d)).reshape(bs, ts, d)
        xp, hp, pp = mixer(xp, start_pos=0)
        xs, hs, ps = mixer(xs, state=state_hgrn[l], hist=state_pool[l], start_pos=past_len)
        xp = ffn2(xp.reshape(bp * tp, d), p=p_prompt[l].reshape(bp * tp, -1)).reshape(bp, tp, d)
        xs = ffn2(xs.reshape(bs * ts, d), p=p_sample[l].reshape(bs * ts, -1)).reshape(bs, ts, d)
        hp_list.append(hp)
        bp_list.append(pp)
        hs_list.append(hs)
        bs_list.append(ps)
    return (xp, xs, jnp.stack(hp_list), jnp.stack(bp_list), jnp.stack(hs_list), jnp.stack(bs_list))
```

```python
import functools

import jax
import jax.numpy as jnp
from jax import lax
from jax.experimental import pallas as pl
from jax.experimental.pallas import tpu as pltpu

EPS = 1e-6
HEAD_DIM = 128
POOL_WINDOWS = (2, 4, 8, 16)
POOL_GDIM = 128
POOL_HIST = 16
POOL_PAD = 8
DECAY_BLK = 64
ATT_SPAN = 256
STREAM_TILE = 512
FFN_TILE = 512
FFN_SUB_ROWS = 512
FF_CHUNK = 256
WEIGHT_STEPS = 16
BF16_SUBLANES = 16
V7X_VMEM_LIMIT_BYTES = 56 * 1024 * 1024

BF16 = jnp.bfloat16
F32 = jnp.float32


def _dot(a, b):
    return jnp.dot(a, b, preferred_element_type=F32)


def _dot_nt(a, b):
    return lax.dot_general(a, b, (((1,), (1,)), ((), ())), preferred_element_type=F32)


def _dot_tn(a, b):
    return lax.dot_general(a, b, (((0,), (0,)), ((), ())), preferred_element_type=F32)


def _rmsnorm(x, g):
    ms = jnp.mean(x * x, axis=-1, keepdims=True)
    return x * lax.rsqrt(ms + EPS) * g


def _silu(x):
    return x * jax.nn.sigmoid(x)


def _resident(shape):
    zeros = (0,) * len(shape)
    return pl.BlockSpec(shape, lambda *_: zeros, pipeline_mode=pl.Buffered(1))


def _staged_kernel(*refs, body, n_lead, n_weights, n_out):
    lead = refs[:n_lead]
    chunks = refs[n_lead:n_lead + n_weights]
    outs = refs[n_lead + n_weights:n_lead + n_weights + n_out]
    rest = refs[n_lead + n_weights + n_out:]
    chunk_outs, resident, scratch = rest[:n_weights], rest[n_weights:2 * n_weights], rest[2 * n_weights:]
    i = pl.program_id(0)

    @pl.when(i < WEIGHT_STEPS)
    def _():
        for chunk, chunk_out, full in zip(chunks, chunk_outs, resident):
            rows = chunk.shape[0]
            w = chunk[...].astype(BF16)
            chunk_out[...] = w
            full[pl.ds(pl.multiple_of(i * rows, rows), rows), :] = w

    @pl.when(i >= WEIGHT_STEPS)
    def _():
        body(*lead, *resident, *outs, *scratch, step=i - WEIGHT_STEPS)


def _staged_call(body, lead, lead_specs, weights, out_specs, out_shape, scratch, name):
    n_tiles = lead_specs[0]
    lead_specs = lead_specs[1:]
    chunk_specs, chunk_out_shapes = [], []
    for w in weights:
        rows, cols = w.shape
        assert rows % (WEIGHT_STEPS * BF16_SUBLANES) == 0, w.shape
        chunk_specs.append(pl.BlockSpec((rows // WEIGHT_STEPS, cols),
                                        lambda i: (jnp.minimum(i, WEIGHT_STEPS - 1), 0)))
        chunk_out_shapes.append(jax.ShapeDtypeStruct(w.shape, BF16))
    result = pl.pallas_call(
        functools.partial(_staged_kernel, body=body, n_lead=len(lead), n_weights=len(weights),
                          n_out=len(out_shape)),
        grid=(WEIGHT_STEPS + n_tiles,),
        in_specs=list(lead_specs) + chunk_specs,
        out_specs=list(out_specs) + chunk_specs,
        out_shape=list(out_shape) + chunk_out_shapes,
        scratch_shapes=[pltpu.VMEM(w.shape, BF16) for w in weights] + list(scratch),
        compiler_params=pltpu.CompilerParams(
            dimension_semantics=("arbitrary",), vmem_limit_bytes=V7X_VMEM_LIMIT_BYTES),
        name=name,
    )(*lead, *weights)
    return result[:len(out_shape)], result[len(out_shape):]


def _tile_step(i):
    return jnp.maximum(i - WEIGHT_STEPS, 0)


def _ffn_body(*refs, sub_rows, pre, post, ple, step=None):
    if ple is None:
        x_ref, gains_ref, wgu_ref, wd_ref, o_ref, act_ref = refs
    else:
        x_ref, p_ref, gains_ref, wgu_ref, wd_ref, wpp_ref, wpg_ref, o_ref, act_ref = refs
    d_ff = wd_ref.shape[0]
    for r0 in range(0, x_ref.shape[0], sub_rows):
        rows = slice(r0, r0 + sub_rows)
        x = x_ref[rows, :]
        h = _rmsnorm(x, gains_ref[pre:pre + 1, :]).astype(BF16)
        for lo in range(0, d_ff, FF_CHUNK):
            a = _dot(h, wgu_ref[:, lo:lo + FF_CHUNK])
            b = _dot(h, wgu_ref[:, d_ff + lo:d_ff + lo + FF_CHUNK])
            act_ref[rows, lo:lo + FF_CHUNK] = (_silu(a) * b).astype(BF16)
        y = _dot(act_ref[rows, :], wd_ref[...])
        x = x + 0.5 * _rmsnorm(y, gains_ref[post:post + 1, :])
        if ple is not None:
            g_pre, g_post = ple
            e = _dot(p_ref[rows, :].astype(BF16), wpp_ref[...])
            gate = jax.nn.sigmoid(_dot(_rmsnorm(x, gains_ref[g_pre:g_pre + 1, :]).astype(BF16), wpg_ref[...]))
            x = x + _rmsnorm(gate * e, gains_ref[g_post:g_post + 1, :])
        o_ref[rows, :] = x


def _ffn(x, gains, weights, *, pre, post, p=None, ple=None, convert):
    n, d = x.shape
    d_ff = weights[1].shape[0]
    tm = min(FFN_TILE, n)
    sub_rows = min(FFN_SUB_ROWS, tm)
    assert n % tm == 0 and tm % sub_rows == 0 and d_ff % FF_CHUNK == 0
    body = functools.partial(_ffn_body, sub_rows=sub_rows, pre=pre, post=post, ple=ple)
    acts = [x] if ple is None else [x, p]
    scratch = [pltpu.VMEM((tm, d_ff), BF16)]
    name = "ffn" if ple is None else "ffn_ple"
    if convert:
        row = lambda width: pl.BlockSpec((tm, width), lambda i: (_tile_step(i), 0))
        (y,), weights_bf = _staged_call(
            body, acts + [gains], [n // tm] + [row(a.shape[1]) for a in acts] + [_resident(gains.shape)],
            list(weights), [row(d)], [jax.ShapeDtypeStruct((n, d), F32)], scratch, name + "_stream")
        return y, weights_bf
    assert n == tm
    whole = lambda a: pl.BlockSpec(a.shape, lambda i: (0, 0))
    y = pl.pallas_call(
        body,
        grid=(1,),
        in_specs=[whole(a) for a in acts] + [_resident(gains.shape)] + [_resident(w.shape) for w in weights],
        out_specs=whole(x),
        out_shape=jax.ShapeDtypeStruct((n, d), F32),
        scratch_shapes=scratch,
        compiler_params=pltpu.CompilerParams(
            dimension_semantics=("arbitrary",), vmem_limit_bytes=V7X_VMEM_LIMIT_BYTES),
        name=name + "_step",
    )(*acts, gains, *weights)
    return y, weights


def _mixer_body(*refs, layer, n_heads, tile, span, chunk, blk, seq_rows, carry, start_pos, g_pre, g_post,
                tiles_per_seq=1, step=0):
    if carry:
        (x_ref, gains_ref, lbl_ref, agn_ref, pscale_ref, win_ref, waup_ref, wpool_ref, wbup_ref, wout_ref,
         o_ref, s_out_ref, pool_out_ref,
         st_scr, hist_scr, lvl_scr, gated_scr, mixed_scr) = refs
        step = step % tiles_per_seq
    else:
        (x_ref, gains_ref, lbl_ref, agn_ref, pscale_ref, win_ref, waup_ref, wpool_ref, wbup_ref, wout_ref,
         s0_ref, hist0_ref, o_ref, s_out_ref, pool_out_ref,
         st_scr, lvl_scr, gated_scr, mixed_scr) = refs
    a_width = n_heads * HEAD_DIM
    b_width = len(POOL_WINDOWS) * POOL_GDIM
    d_model = x_ref.shape[-1]
    n_seq = tile // seq_rows
    n_chunks = tile // chunk
    ext_rows = POOL_PAD + POOL_HIST + seq_rows

    if carry:
        @pl.when(step == 0)
        def _():
            st_scr[...] = jnp.zeros_like(st_scr)
            hist_scr[...] = jnp.zeros_like(hist_scr)
    else:
        for s in range(n_seq):
            for hd in range(n_heads):
                st_scr[s * n_heads + hd] = s0_ref[s, hd].T

    x = x_ref[...]
    h = _rmsnorm(x, gains_ref[g_pre:g_pre + 1, :]).astype(BF16)

    logits = lbl_ref[...]
    ex = jnp.exp(logits - jnp.max(logits, axis=0, keepdims=True))
    lb = jnp.sum(ex[:layer + 1], axis=0, keepdims=True) / jnp.sum(ex, axis=0, keepdims=True)

    r_idx = lax.broadcasted_iota(jnp.int32, (span, span), 0)
    c_idx = lax.broadcasted_iota(jnp.int32, (span, span), 1)
    tri = jnp.where((c_idx <= r_idx) & (c_idx >= (r_idx // blk) * blk), 1.0, 0.0).astype(BF16)
    causal = (c_idx <= r_idx) & (c_idx >= (r_idx // chunk) * chunk)

    heads = [slice(hd * HEAD_DIM, (hd + 1) * HEAD_DIM) for hd in range(n_heads)]
    blocks = [slice(c * chunk, (c + 1) * chunk) for c in range(n_chunks)]
    subs = [slice(b * blk, (b + 1) * blk) for b in range(tile // blk)]
    two_level = chunk == 2 * blk
    assert two_level or chunk == blk
    proj = lambda lo_col, width: _dot(h, win_ref[:, lo_col:lo_col + width])
    fz = proj(a_width, a_width)
    q = proj(0, a_width)
    e = jnp.exp(-jnp.abs(fz))
    r = 1.0 / (1.0 + e)
    kk = (1.0 - lb) * jnp.where(fz >= 0.0, e * r, r)
    logf = jnp.log2(1.0 - kk)
    hi = logf.astype(BF16)
    lo = (logf - hi.astype(F32)).astype(BF16)
    v = proj(2 * a_width, a_width).astype(BF16)
    u = proj(4 * a_width, b_width)
    spans = [slice(s * span, (s + 1) * span) for s in range(tile // span)]
    g_parts = [_dot(tri, hi[sp]) + _dot(tri, lo[sp]) for sp in spans]
    g_cum = jnp.concatenate(g_parts, axis=0) if len(spans) > 1 else g_parts[0]
    og = proj(3 * a_width, a_width)
    ga = proj(4 * a_width + b_width, d_model)
    gb = proj(4 * a_width + b_width + d_model, d_model)
    rows_cat = lambda parts: jnp.concatenate(parts, axis=0) if len(parts) > 1 else parts[0]
    dec = jnp.exp2(g_cum)
    sub_dec = [dec[sb.stop - 1:sb.stop, :] for sb in subs]
    q_loc = _silu(q) * dec
    k_loc = kk * jnp.exp2(-g_cum)
    k_fwd = rows_cat([k_loc[sb] * sub_dec[b] for b, sb in enumerate(subs)])
    if two_level:
        q_chunk = rows_cat([q_loc[sb] * sub_dec[b - 1] if b % 2 else q_loc[sb] for b, sb in enumerate(subs)])
        k_end = rows_cat([k_fwd[sb] if b % 2 else k_fwd[sb] * sub_dec[b + 1] for b, sb in enumerate(subs)])
        d_chunk = [sub_dec[2 * c] * sub_dec[2 * c + 1] for c in range(n_chunks)]
    else:
        q_chunk, k_end, d_chunk = q_loc, k_fwd, sub_dec
    q_loc_bf, k_loc_bf = q_loc.astype(BF16), k_loc.astype(BF16)
    q_chunk_bf, k_end_bf = q_chunk.astype(BF16), k_end.astype(BF16)
    if two_level:
        k_fwd_bf = k_fwd.astype(BF16)
        zero = jnp.zeros((blk, HEAD_DIM), BF16)

        def scores(sp, ls):
            bs = range(sp.start // blk, sp.stop // blk)
            lhs = jnp.concatenate(
                [rows_cat([zero if b % 2 else q_loc_bf[subs[b], ls] for b in bs]),
                 rows_cat([q_loc_bf[subs[b], ls] if b % 2 else zero for b in bs])], axis=1)
            rhs = jnp.concatenate(
                [rows_cat([zero if b % 2 else k_loc_bf[subs[b], ls] for b in bs]),
                 rows_cat([k_loc_bf[subs[b], ls] if b % 2 else k_fwd_bf[subs[b], ls] for b in bs])], axis=1)
            return _dot_nt(lhs, rhs)
    else:
        scores = lambda sp, ls: _dot_nt(q_loc_bf[sp, ls], k_loc_bf[sp, ls])
    att = [[jnp.where(causal, scores(sp, ls), 0.0).astype(BF16) for ls in heads] for sp in spans]
    o_intra = [[_dot(att[s][hd], v[sp, ls]) for hd, ls in enumerate(heads)] for s, sp in enumerate(spans)]
    kv = [[_dot_tn(v[rows, ls], k_end_bf[rows, ls]) for ls in heads] for rows in blocks]
    o_heads = [[] for _ in range(n_heads)]
    state = {}
    for c, rows in enumerate(blocks):
        for hd, ls in enumerate(heads):
            s_idx = hd if carry else (rows.start // seq_rows) * n_heads + hd
            st = state[s_idx] if s_idx in state else st_scr[s_idx]
            s, lo_row = divmod(rows.start, span)
            o_heads[hd].append(o_intra[s][hd][lo_row:lo_row + chunk]
                               + _dot_nt(q_chunk_bf[rows, ls], st.astype(BF16)))
            state[s_idx] = st * d_chunk[c][:, ls] + kv[c][hd]
    for s_idx, st in state.items():
        st_scr[s_idx] = st
    gated_og = _silu(og)
    for hd, ls in enumerate(heads):
        o = jnp.concatenate(o_heads[hd], axis=0) if n_chunks > 1 else o_heads[hd][0]
        o = o * lax.rsqrt(jnp.mean(o * o, axis=-1, keepdims=True) + EPS) * agn_ref[...]
        gated_scr[:, ls] = (o * gated_og[:, ls]).astype(BF16)
    y_a = _dot(gated_scr[...], waup_ref[...])

    row = lax.broadcasted_iota(jnp.int32, (seq_rows, 1), 0)
    first_pos = start_pos + (step * tile if carry else 0)
    n_levels = len(POOL_WINDOWS)
    front = POOL_PAD + POOL_HIST
    for s in range(n_seq):
        base = s * ext_rows
        for k in range(n_levels - 1):
            lvl_scr[k, base:base + POOL_PAD, :] = jnp.zeros((POOL_PAD, b_width), F32)
        if carry:
            lvl_scr[0, base + POOL_PAD:base + front, :] = hist_scr[...]
        else:
            lvl_scr[0, base + POOL_PAD:base + POOL_PAD + 1, :] = jnp.zeros((1, b_width), F32)
            lvl_scr[0, base + POOL_PAD + 1:base + front, :] = hist0_ref[s]
        lvl_scr[0, base + front:base + ext_rows, :] = u[s * seq_rows:(s + 1) * seq_rows]
    for s in range(n_seq):
        base = s * ext_rows
        out_rows = slice(s * seq_rows, (s + 1) * seq_rows)
        for k in range(1, n_levels + 1):
            g, w, shift = k - 1, POOL_WINDOWS[k - 1], 2 ** (k - 1)
            assert w == 2 * shift
            lo_col = g * POOL_GDIM
            cols = slice(lo_col, lo_col + POOL_GDIM)
            lo_row = base + (front if k == n_levels else POOL_PAD)
            summed = (lvl_scr[k - 1, lo_row:base + ext_rows, lo_col:]
                      + lvl_scr[k - 1, lo_row - shift:base + ext_rows - shift, lo_col:])
            if k < n_levels:
                lvl_scr[k, lo_row:base + ext_rows, lo_col:] = summed
                summed = summed[POOL_HIST:, :POOL_GDIM]
            cur = lvl_scr[0, base + front:base + ext_rows, cols]
            cnt = jnp.minimum(w, first_pos + row + 1).astype(F32)
            pooled = summed / cnt - cur
            mixed = _dot(pooled.astype(BF16), wpool_ref[cols, :]) * pscale_ref[:, cols]
            mixed_scr[out_rows, cols] = mixed.astype(BF16)
        last = lvl_scr[0, base + ext_rows - POOL_HIST:base + ext_rows, :]
        if carry:
            hist_scr[...] = last
            pool_out_ref[...] = last
        else:
            pool_out_ref[s] = last
    y_b = _dot(mixed_scr[...], wbup_ref[...])

    m = (jax.nn.sigmoid(ga) * y_a + jax.nn.sigmoid(gb) * y_b).astype(BF16)
    o_ref[...] = x + _rmsnorm(_dot(m, wout_ref[...]), gains_ref[g_post:g_post + 1, :])

    if carry:
        @pl.when(step == tiles_per_seq - 1)
        def _():
            for hd in range(n_heads):
                s_out_ref[hd] = st_scr[hd].T
    else:
        for s in range(n_seq):
            for hd in range(n_heads):
                s_out_ref[s, hd] = st_scr[s * n_heads + hd].T


def _mixer(x, smalls, weights, *, layer, state=None, hist=None, start_pos, g_pre, g_post):
    bsz, t_len, d = x.shape
    n_heads = weights[1].shape[0] // HEAD_DIM
    b_width = weights[3].shape[0]
    carry = state is None
    if carry:
        tile = min(STREAM_TILE, t_len)
        blk = min(DECAY_BLK, tile)
        chunk = min(2 * blk, tile)
        assert t_len % tile == 0 and tile % chunk == 0 and tile >= POOL_HIST
        seq_rows, n_seq, n_state = tile, 1, n_heads
    else:
        tile = bsz * t_len
        chunk = blk = seq_rows = t_len
        assert t_len % BF16_SUBLANES == 0 and t_len >= POOL_HIST
        n_seq, n_state = bsz, bsz * n_heads
    span = min(ATT_SPAN, tile)
    assert tile % span == 0 and span % chunk == 0
    tiles_per_seq = t_len // tile if carry else 1
    body = functools.partial(_mixer_body, layer=layer, n_heads=n_heads, tile=tile, span=span, chunk=chunk,
                             blk=blk, seq_rows=seq_rows, carry=carry, start_pos=start_pos,
                             g_pre=g_pre, g_post=g_post, tiles_per_seq=tiles_per_seq)
    scratch = [pltpu.VMEM((n_state, HEAD_DIM, HEAD_DIM), F32)]
    if carry:
        scratch.append(pltpu.VMEM((POOL_HIST, b_width), F32))
    scratch += [
        pltpu.VMEM((len(POOL_WINDOWS), n_seq * (POOL_PAD + POOL_HIST + seq_rows), b_width), F32),
        pltpu.VMEM((tile, n_heads * HEAD_DIM), BF16),
        pltpu.VMEM((tile, b_width), BF16),
    ]
    small_specs = [_resident(a.shape) for a in smalls]
    if carry:
        seq_of = lambda i: _tile_step(i) // tiles_per_seq
        tile_of = lambda i: _tile_step(i) % tiles_per_seq
        out_specs = [
            pl.BlockSpec((None, tile, d), lambda i: (seq_of(i), tile_of(i), 0)),
            pl.BlockSpec((None, n_heads, HEAD_DIM, HEAD_DIM), lambda i: (seq_of(i), 0, 0, 0)),
            pl.BlockSpec((None, POOL_HIST, b_width), lambda i: (seq_of(i), 0, 0)),
        ]
        out_shape = [
            jax.ShapeDtypeStruct((bsz, t_len, d), F32),
            jax.ShapeDtypeStruct((bsz, n_heads, HEAD_DIM, HEAD_DIM), F32),
            jax.ShapeDtypeStruct((bsz, POOL_HIST, b_width), F32),
        ]
        (y, s_new, pool_new), weights_bf = _staged_call(
            body, [x] + list(smalls),
            [bsz * tiles_per_seq, pl.BlockSpec((None, tile, d), lambda i: (seq_of(i), tile_of(i), 0))] + small_specs,
            list(weights), out_specs, out_shape, scratch, "mixer_stream")
    else:
        out_shape = [
            jax.ShapeDtypeStruct((tile, d), F32),
            jax.ShapeDtypeStruct(state.shape, F32),
            jax.ShapeDtypeStruct((bsz, POOL_HIST, b_width), F32),
        ]
        whole = lambda shape: pl.BlockSpec(shape, lambda i: (0,) * len(shape))
        y, s_new, pool_new = pl.pallas_call(
            body,
            grid=(1,),
            in_specs=([whole((tile, d))] + small_specs + [_resident(w.shape) for w in weights]
                      + [_resident(state.shape), _resident(hist.shape)]),
            out_specs=[whole(s.shape) for s in out_shape],
            out_shape=out_shape,
            scratch_shapes=scratch,
            compiler_params=pltpu.CompilerParams(
                dimension_semantics=("arbitrary",), vmem_limit_bytes=V7X_VMEM_LIMIT_BYTES),
            name="mixer_step",
        )(x.reshape(tile, d), *smalls, *weights, state, hist)
        weights_bf = weights
    return y.reshape(bsz, t_len, d), s_new, pool_new[:, 1:, :], weights_bf


def kernel(x_prompt, x_sample, p_prompt, p_sample, state_hgrn, state_pool, norm_gains, lb_logits, w_ffn1_gu, w_ffn1_down, w_in, a_gnorm, w_a_up, w_pool, pool_scale, w_b_up, w_out, w_ffn2_gu, w_ffn2_down, w_ple_proj, w_ple_gate):
    depth = w_in.shape[0]
    past_len = 2048
    bp, tp, d = x_prompt.shape
    bs, ts, _ = x_sample.shape
    xp, xs = x_prompt.reshape(bp * tp, d), x_sample.reshape(bs * ts, d)
    hp_list, bp_list, hs_list, bs_list = [], [], [], []
    for l in range(depth):
        gains = norm_gains[l]
        xp, w1 = _ffn(xp, gains, (w_ffn1_gu[l], w_ffn1_down[l]), pre=0, post=1, convert=True)
        xs, _ = _ffn(xs, gains, w1, pre=0, post=1, convert=False)

        smalls = (gains, lb_logits, a_gnorm[l].reshape(1, -1), pool_scale[l].reshape(1, -1))
        mix_w = (w_in[l], w_a_up[l], w_pool[l].reshape(-1, POOL_GDIM), w_b_up[l], w_out[l])
        mix = functools.partial(_mixer, smalls=smalls, layer=l, g_pre=2, g_post=3)
        xp3, hp, pp, mix_w_bf = mix(xp.reshape(bp, tp, d), weights=mix_w, start_pos=0)
        xs3, hs, ps, _ = mix(xs.reshape(bs, ts, d), weights=mix_w_bf, state=state_hgrn[l],
                             hist=state_pool[l], start_pos=past_len)

        w2 = (w_ffn2_gu[l], w_ffn2_down[l], w_ple_proj[l], w_ple_gate[l])
        xp, w2_bf = _ffn(xp3.reshape(bp * tp, d), gains, w2, pre=4, post=5, ple=(6, 7),
                         p=p_prompt[l].reshape(bp * tp, -1), convert=True)
        xs, _ = _ffn(xs3.reshape(bs * ts, d), gains, w2_bf, pre=4, post=5, ple=(6, 7),
                     p=p_sample[l].reshape(bs * ts, -1), convert=False)
        hp_list.append(hp)
        bp_list.append(pp)
        hs_list.append(hs)
        bs_list.append(ps)
    return (xp.reshape(bp, tp, d), xs.reshape(bs, ts, d), jnp.stack(hp_list), jnp.stack(bp_list),
            jnp.stack(hs_list), jnp.stack(bs_list))
```

```python
import functools

import jax
import jax.numpy as jnp
from jax import lax
from jax.experimental import pallas as pl
from jax.experimental.pallas import tpu as pltpu

EPS = 1e-6
HEAD_DIM = 128
POOL_WINDOWS = (2, 4, 8, 16)
POOL_GDIM = 128
POOL_HIST = 16
POOL_PAD = 8
DECAY_BLK = 64
ATT_SPAN = 256
STREAM_TILE = 512
FFN_TILE = 512
FFN_SUB_ROWS = 512
FF_CHUNK = 256
WEIGHT_STEPS = 16
BF16_SUBLANES = 16
V7X_VMEM_LIMIT_BYTES = 56 * 1024 * 1024

BF16 = jnp.bfloat16
F32 = jnp.float32


def _dot(a, b):
    return jnp.dot(a, b, preferred_element_type=F32)


def _dot_nt(a, b):
    return lax.dot_general(a, b, (((1,), (1,)), ((), ())), preferred_element_type=F32)


def _dot_tn(a, b):
    return lax.dot_general(a, b, (((0,), (0,)), ((), ())), preferred_element_type=F32)


def _rmsnorm(x, g):
    ms = jnp.mean(x * x, axis=-1, keepdims=True)
    return x * lax.rsqrt(ms + EPS) * g


def _silu(x):
    return x * jax.nn.sigmoid(x)


def _resident(shape):
    zeros = (0,) * len(shape)
    return pl.BlockSpec(shape, lambda *_: zeros, pipeline_mode=pl.Buffered(1))


def _stream_kernel(*refs, body, n_lead, n_own, n_out, side_steps, own_f32):
    refs = list(refs)
    take = lambda n: [refs.pop(0) for _ in range(n)]
    lead, own_in, side_in, outs = take(n_lead), take(n_own), take(len(side_steps)), take(n_out)
    own_out = take(n_own) if own_f32 else []
    side_out = take(len(side_steps))
    resident = take(n_own) if own_f32 else own_in
    scratch = refs
    i = pl.program_id(0)

    def tile_step():
        t = i - WEIGHT_STEPS if own_f32 else i
        for chunk, chunk_out, steps in zip(side_in, side_out, side_steps):
            @pl.when(t < steps)
            def _():
                chunk_out[...] = chunk[...].astype(BF16)
        body(*lead, *resident, *outs, *scratch, step=t)

    if own_f32:
        @pl.when(i < WEIGHT_STEPS)
        def _():
            for chunk, chunk_out, full in zip(own_in, own_out, resident):
                rows = chunk.shape[0]
                w = chunk[...].astype(BF16)
                chunk_out[...] = w
                full[pl.ds(pl.multiple_of(i * rows, rows), rows), :] = w

        pl.when(i >= WEIGHT_STEPS)(tile_step)
    else:
        tile_step()


def _chunk_steps(rows, max_steps):
    for steps in range(max_steps, 0, -1):
        if rows % (steps * BF16_SUBLANES) == 0:
            return steps
    raise ValueError(f"{rows} rows do not split into bf16 tiles")


def _stream_call(body, name, n_tiles, lead, lead_specs, weights, own_f32, side_weights,
                 out_specs, out_shape, scratch):
    first = WEIGHT_STEPS if own_f32 else 0
    ts = (lambda i: jnp.maximum(i - first, 0)) if own_f32 else (lambda i: i)

    def chunked(w, steps, step_of):
        rows, cols = w.shape
        return pl.BlockSpec((rows // steps, cols), lambda i: (jnp.minimum(step_of(i), steps - 1), 0))

    if own_f32:
        assert all(_chunk_steps(w.shape[0], WEIGHT_STEPS) == WEIGHT_STEPS for w in weights)
        own_specs = [chunked(w, WEIGHT_STEPS, lambda i: i) for w in weights]
    else:
        own_specs = [_resident(w.shape) for w in weights]
    side_steps = tuple(_chunk_steps(w.shape[0], n_tiles) for w in side_weights)
    side_specs = [chunked(w, s, ts) for w, s in zip(side_weights, side_steps)]
    as_bf16 = lambda ws: [jax.ShapeDtypeStruct(w.shape, BF16) for w in ws]
    result = pl.pallas_call(
        functools.partial(_stream_kernel, body=body, n_lead=len(lead), n_own=len(weights),
                          n_out=len(out_shape), side_steps=side_steps, own_f32=own_f32),
        grid=(first + n_tiles,),
        in_specs=[spec(ts) for spec in lead_specs] + own_specs + side_specs,
        out_specs=[spec(ts) for spec in out_specs] + (own_specs if own_f32 else []) + side_specs,
        out_shape=list(out_shape) + (as_bf16(weights) if own_f32 else []) + as_bf16(side_weights),
        scratch_shapes=([pltpu.VMEM(w.shape, BF16) for w in weights] if own_f32 else []) + list(scratch),
        compiler_params=pltpu.CompilerParams(
            dimension_semantics=("arbitrary",), vmem_limit_bytes=V7X_VMEM_LIMIT_BYTES),
        name=name,
    )(*lead, *weights, *side_weights)
    result = list(result)
    outs = [result.pop(0) for _ in out_shape]
    own_bf = [result.pop(0) for _ in weights] if own_f32 else list(weights)
    return outs, own_bf, result


def _ffn_body(*refs, sub_rows, pre, post, ple, step=None):
    if ple is None:
        x_ref, gains_ref, wgu_ref, wd_ref, o_ref, act_ref = refs
    else:
        x_ref, p_ref, gains_ref, wgu_ref, wd_ref, wpp_ref, wpg_ref, o_ref, act_ref = refs
    d_ff = wd_ref.shape[0]
    for r0 in range(0, x_ref.shape[0], sub_rows):
        rows = slice(r0, r0 + sub_rows)
        x = x_ref[rows, :]
        h = _rmsnorm(x, gains_ref[pre:pre + 1, :]).astype(BF16)
        for lo in range(0, d_ff, FF_CHUNK):
            a = _dot(h, wgu_ref[:, lo:lo + FF_CHUNK])
            b = _dot(h, wgu_ref[:, d_ff + lo:d_ff + lo + FF_CHUNK])
            act_ref[rows, lo:lo + FF_CHUNK] = (_silu(a) * b).astype(BF16)
        y = _dot(act_ref[rows, :], wd_ref[...])
        x = x + 0.5 * _rmsnorm(y, gains_ref[post:post + 1, :])
        if ple is not None:
            g_pre, g_post = ple
            e = _dot(p_ref[rows, :].astype(BF16), wpp_ref[...])
            gate = jax.nn.sigmoid(_dot(_rmsnorm(x, gains_ref[g_pre:g_pre + 1, :]).astype(BF16), wpg_ref[...]))
            x = x + _rmsnorm(gate * e, gains_ref[g_post:g_post + 1, :])
        o_ref[rows, :] = x


def _ffn(x, gains, weights, *, pre, post, p=None, ple=None, own_f32=False, side_weights=()):
    n, d = x.shape
    d_ff = weights[1].shape[0]
    tm = min(FFN_TILE, n)
    sub_rows = min(FFN_SUB_ROWS, tm)
    assert n % tm == 0 and tm % sub_rows == 0 and d_ff % FF_CHUNK == 0
    body = functools.partial(_ffn_body, sub_rows=sub_rows, pre=pre, post=post, ple=ple)
    acts = [x] if ple is None else [x, p]
    row = lambda width: lambda ts: pl.BlockSpec((tm, width), lambda i: (ts(i), 0))
    (y,), own_bf, side_bf = _stream_call(
        body, "ffn" if ple is None else "ffn_ple", n // tm,
        acts + [gains], [row(a.shape[1]) for a in acts] + [lambda ts: _resident(gains.shape)],
        list(weights), own_f32, list(side_weights),
        [row(d)], [jax.ShapeDtypeStruct((n, d), F32)], [pltpu.VMEM((tm, d_ff), BF16)])
    return y, own_bf, side_bf


def _mixer_body(*refs, layer, n_heads, tile, span, chunk, blk, seq_rows, carry, start_pos, g_pre, g_post,
                tiles_per_seq=1, step=0):
    if carry:
        (x_ref, gains_ref, lbl_ref, agn_ref, pscale_ref, win_ref, waup_ref, wpool_ref, wbup_ref, wout_ref,
         o_ref, s_out_ref, pool_out_ref,
         st_scr, hist_scr, lvl_scr, gated_scr, mixed_scr) = refs
        step = step % tiles_per_seq
    else:
        (x_ref, s0_ref, hist0_ref, gains_ref, lbl_ref, agn_ref, pscale_ref,
         win_ref, waup_ref, wpool_ref, wbup_ref, wout_ref,
         o_ref, s_out_ref, pool_out_ref,
         st_scr, lvl_scr, gated_scr, mixed_scr) = refs
    a_width = n_heads * HEAD_DIM
    b_width = len(POOL_WINDOWS) * POOL_GDIM
    d_model = x_ref.shape[-1]
    n_seq = tile // seq_rows
    n_chunks = tile // chunk
    ext_rows = POOL_PAD + POOL_HIST + seq_rows

    if carry:
        @pl.when(step == 0)
        def _():
            st_scr[...] = jnp.zeros_like(st_scr)
            hist_scr[...] = jnp.zeros_like(hist_scr)
    else:
        for s in range(n_seq):
            for hd in range(n_heads):
                st_scr[s * n_heads + hd] = s0_ref[s, hd].T

    x = x_ref[...]
    h = _rmsnorm(x, gains_ref[g_pre:g_pre + 1, :]).astype(BF16)

    logits = lbl_ref[...]
    ex = jnp.exp(logits - jnp.max(logits, axis=0, keepdims=True))
    lb = jnp.sum(ex[:layer + 1], axis=0, keepdims=True) / jnp.sum(ex, axis=0, keepdims=True)

    r_idx = lax.broadcasted_iota(jnp.int32, (span, span), 0)
    c_idx = lax.broadcasted_iota(jnp.int32, (span, span), 1)
    tri = jnp.where((c_idx <= r_idx) & (c_idx >= (r_idx // blk) * blk), 1.0, 0.0).astype(BF16)
    causal = (c_idx <= r_idx) & (c_idx >= (r_idx // chunk) * chunk)

    heads = [slice(hd * HEAD_DIM, (hd + 1) * HEAD_DIM) for hd in range(n_heads)]
    blocks = [slice(c * chunk, (c + 1) * chunk) for c in range(n_chunks)]
    subs = [slice(b * blk, (b + 1) * blk) for b in range(tile // blk)]
    two_level = chunk == 2 * blk
    assert two_level or chunk == blk
    proj = lambda lo_col, width: _dot(h, win_ref[:, lo_col:lo_col + width])
    fz = proj(a_width, a_width)
    q = proj(0, a_width)
    e = jnp.exp(-jnp.abs(fz))
    r = 1.0 / (1.0 + e)
    kk = (1.0 - lb) * jnp.where(fz >= 0.0, e * r, r)
    logf = jnp.log2(1.0 - kk)
    hi = logf.astype(BF16)
    lo = (logf - hi.astype(F32)).astype(BF16)
    v = proj(2 * a_width, a_width).astype(BF16)
    u = proj(4 * a_width, b_width)
    spans = [slice(s * span, (s + 1) * span) for s in range(tile // span)]
    g_parts = [_dot(tri, hi[sp]) + _dot(tri, lo[sp]) for sp in spans]
    g_cum = jnp.concatenate(g_parts, axis=0) if len(spans) > 1 else g_parts[0]
    og = proj(3 * a_width, a_width)
    ga = proj(4 * a_width + b_width, d_model)
    gb = proj(4 * a_width + b_width + d_model, d_model)
    rows_cat = lambda parts: jnp.concatenate(parts, axis=0) if len(parts) > 1 else parts[0]
    dec = jnp.exp2(g_cum)
    sub_dec = [dec[sb.stop - 1:sb.stop, :] for sb in subs]
    q_loc = _silu(q) * dec
    k_loc = kk * jnp.exp2(-g_cum)
    k_fwd = rows_cat([k_loc[sb] * sub_dec[b] for b, sb in enumerate(subs)])
    if two_level:
        q_chunk = rows_cat([q_loc[sb] * sub_dec[b - 1] if b % 2 else q_loc[sb] for b, sb in enumerate(subs)])
        k_end = rows_cat([k_fwd[sb] if b % 2 else k_fwd[sb] * sub_dec[b + 1] for b, sb in enumerate(subs)])
        d_chunk = [sub_dec[2 * c] * sub_dec[2 * c + 1] for c in range(n_chunks)]
    else:
        q_chunk, k_end, d_chunk = q_loc, k_fwd, sub_dec
    q_loc_bf, k_loc_bf = q_loc.astype(BF16), k_loc.astype(BF16)
    q_chunk_bf, k_end_bf = q_chunk.astype(BF16), k_end.astype(BF16)
    if two_level:
        k_fwd_bf = k_fwd.astype(BF16)
        zero = jnp.zeros((blk, HEAD_DIM), BF16)

        def scores(sp, ls):
            bs = range(sp.start // blk, sp.stop // blk)
            lhs = jnp.concatenate(
                [rows_cat([zero if b % 2 else q_loc_bf[subs[b], ls] for b in bs]),
                 rows_cat([q_loc_bf[subs[b], ls] if b % 2 else zero for b in bs])], axis=1)
            rhs = jnp.concatenate(
                [rows_cat([zero if b % 2 else k_loc_bf[subs[b], ls] for b in bs]),
                 rows_cat([k_loc_bf[subs[b], ls] if b % 2 else k_fwd_bf[subs[b], ls] for b in bs])], axis=1)
            return _dot_nt(lhs, rhs)
    else:
        scores = lambda sp, ls: _dot_nt(q_loc_bf[sp, ls], k_loc_bf[sp, ls])
    att = [[jnp.where(causal, scores(sp, ls), 0.0).astype(BF16) for ls in heads] for sp in spans]
    o_intra = [[_dot(att[s][hd], v[sp, ls]) for hd, ls in enumerate(heads)] for s, sp in enumerate(spans)]
    kv = [[_dot_tn(v[rows, ls], k_end_bf[rows, ls]) for ls in heads] for rows in blocks]
    o_heads = [[] for _ in range(n_heads)]
    state = {}
    for c, rows in enumerate(blocks):
        for hd, ls in enumerate(heads):
            s_idx = hd if carry else (rows.start // seq_rows) * n_heads + hd
            st = state[s_idx] if s_idx in state else st_scr[s_idx]
            s, lo_row = divmod(rows.start, span)
            o_heads[hd].append(o_intra[s][hd][lo_row:lo_row + chunk]
                               + _dot_nt(q_chunk_bf[rows, ls], st.astype(BF16)))
            state[s_idx] = st * d_chunk[c][:, ls] + kv[c][hd]
    for s_idx, st in state.items():
        st_scr[s_idx] = st
    gated_og = _silu(og)
    for hd, ls in enumerate(heads):
        o = jnp.concatenate(o_heads[hd], axis=0) if n_chunks > 1 else o_heads[hd][0]
        o = o * lax.rsqrt(jnp.mean(o * o, axis=-1, keepdims=True) + EPS) * agn_ref[...]
        gated_scr[:, ls] = (o * gated_og[:, ls]).astype(BF16)
    y_a = _dot(gated_scr[...], waup_ref[...])

    row = lax.broadcasted_iota(jnp.int32, (seq_rows, 1), 0)
    first_pos = start_pos + (step * tile if carry else 0)
    n_levels = len(POOL_WINDOWS)
    front = POOL_PAD + POOL_HIST
    for s in range(n_seq):
        base = s * ext_rows
        for k in range(n_levels - 1):
            lvl_scr[k, base:base + POOL_PAD, :] = jnp.zeros((POOL_PAD, b_width), F32)
        if carry:
            lvl_scr[0, base + POOL_PAD:base + front, :] = hist_scr[...]
        else:
            lvl_scr[0, base + POOL_PAD:base + POOL_PAD + 1, :] = jnp.zeros((1, b_width), F32)
            lvl_scr[0, base + POOL_PAD + 1:base + front, :] = hist0_ref[s]
        lvl_scr[0, base + front:base + ext_rows, :] = u[s * seq_rows:(s + 1) * seq_rows]
    for s in range(n_seq):
        base = s * ext_rows
        out_rows = slice(s * seq_rows, (s + 1) * seq_rows)
        for k in range(1, n_levels + 1):
            g, w, shift = k - 1, POOL_WINDOWS[k - 1], 2 ** (k - 1)
            assert w == 2 * shift
            lo_col = g * POOL_GDIM
            cols = slice(lo_col, lo_col + POOL_GDIM)
            lo_row = base + (front if k == n_levels else POOL_PAD)
            summed = (lvl_scr[k - 1, lo_row:base + ext_rows, lo_col:]
                      + lvl_scr[k - 1, lo_row - shift:base + ext_rows - shift, lo_col:])
            if k < n_levels:
                lvl_scr[k, lo_row:base + ext_rows, lo_col:] = summed
                summed = summed[POOL_HIST:, :POOL_GDIM]
            cur = lvl_scr[0, base + front:base + ext_rows, cols]
            cnt = jnp.minimum(w, first_pos + row + 1).astype(F32)
            pooled = summed / cnt - cur
            mixed = _dot(pooled.astype(BF16), wpool_ref[cols, :]) * pscale_ref[:, cols]
            mixed_scr[out_rows, cols] = mixed.astype(BF16)
        last = lvl_scr[0, base + ext_rows - POOL_HIST:base + ext_rows, :]
        if carry:
            hist_scr[...] = last
            pool_out_ref[...] = last
        else:
            pool_out_ref[s] = last
    y_b = _dot(mixed_scr[...], wbup_ref[...])

    m = (jax.nn.sigmoid(ga) * y_a + jax.nn.sigmoid(gb) * y_b).astype(BF16)
    o_ref[...] = x + _rmsnorm(_dot(m, wout_ref[...]), gains_ref[g_post:g_post + 1, :])

    if carry:
        @pl.when(step == tiles_per_seq - 1)
        def _():
            for hd in range(n_heads):
                s_out_ref[hd] = st_scr[hd].T
    else:
        for s in range(n_seq):
            for hd in range(n_heads):
                s_out_ref[s, hd] = st_scr[s * n_heads + hd].T


def _mixer(x, smalls, weights, *, layer, state=None, hist=None, start_pos, g_pre, g_post,
           own_f32=False, side_weights=()):
    bsz, t_len, d = x.shape
    n_heads = weights[1].shape[0] // HEAD_DIM
    b_width = weights[3].shape[0]
    carry = state is None
    if carry:
        tile = min(STREAM_TILE, t_len)
        blk = min(DECAY_BLK, tile)
        chunk = min(2 * blk, tile)
        assert t_len % tile == 0 and tile % chunk == 0 and tile >= POOL_HIST
        seq_rows, n_seq, n_state = tile, 1, n_heads
    else:
        tile = bsz * t_len
        chunk = blk = seq_rows = t_len
        assert t_len % BF16_SUBLANES == 0 and t_len >= POOL_HIST
        n_seq, n_state = bsz, bsz * n_heads
    span = min(ATT_SPAN, tile)
    assert tile % span == 0 and span % chunk == 0
    tiles_per_seq = t_len // tile if carry else 1
    body = functools.partial(_mixer_body, layer=layer, n_heads=n_heads, tile=tile, span=span, chunk=chunk,
                             blk=blk, seq_rows=seq_rows, carry=carry, start_pos=start_pos,
                             g_pre=g_pre, g_post=g_post, tiles_per_seq=tiles_per_seq)
    scratch = [pltpu.VMEM((n_state, HEAD_DIM, HEAD_DIM), F32)]
    if carry:
        scratch.append(pltpu.VMEM((POOL_HIST, b_width), F32))
    scratch += [
        pltpu.VMEM((len(POOL_WINDOWS), n_seq * (POOL_PAD + POOL_HIST + seq_rows), b_width), F32),
        pltpu.VMEM((tile, n_heads * HEAD_DIM), BF16),
        pltpu.VMEM((tile, b_width), BF16),
    ]
    small_specs = [(lambda ts, shape=a.shape: _resident(shape)) for a in smalls]
    if carry:
        n_tiles = bsz * tiles_per_seq
        seq_tile = lambda ts: lambda i: (ts(i) // tiles_per_seq, ts(i) % tiles_per_seq, 0)
        per_seq = lambda ts: lambda i: (ts(i) // tiles_per_seq, 0, 0)
        lead = [x] + list(smalls)
        lead_specs = [lambda ts: pl.BlockSpec((None, tile, d), seq_tile(ts))] + small_specs
        out_specs = [
            lambda ts: pl.BlockSpec((None, tile, d), seq_tile(ts)),
            lambda ts: pl.BlockSpec((None, n_heads, HEAD_DIM, HEAD_DIM), lambda i: per_seq(ts)(i) + (0,)),
            lambda ts: pl.BlockSpec((None, POOL_HIST, b_width), per_seq(ts)),
        ]
        out_shape = [
            jax.ShapeDtypeStruct((bsz, t_len, d), F32),
            jax.ShapeDtypeStruct((bsz, n_heads, HEAD_DIM, HEAD_DIM), F32),
            jax.ShapeDtypeStruct((bsz, POOL_HIST, b_width), F32),
        ]
    else:
        n_tiles = 1
        whole = lambda shape: lambda ts: pl.BlockSpec(shape, lambda i: (0,) * len(shape))
        lead = [x.reshape(tile, d), state, hist] + list(smalls)
        lead_specs = ([whole((tile, d)), lambda ts: _resident(state.shape), lambda ts: _resident(hist.shape)]
                      + small_specs)
        out_shape = [
            jax.ShapeDtypeStruct((tile, d), F32),
            jax.ShapeDtypeStruct(state.shape, F32),
            jax.ShapeDtypeStruct((bsz, POOL_HIST, b_width), F32),
        ]
        out_specs = [whole(s.shape) for s in out_shape]
    (y, s_new, pool_new), own_bf, side_bf = _stream_call(
        body, "mixer", n_tiles, lead, lead_specs, list(weights), own_f32, list(side_weights),
        out_specs, out_shape, scratch)
    return y.reshape(bsz, t_len, d), s_new, pool_new[:, 1:, :], own_bf, side_bf


def kernel(x_prompt, x_sample, p_prompt, p_sample, state_hgrn, state_pool, norm_gains, lb_logits, w_ffn1_gu, w_ffn1_down, w_in, a_gnorm, w_a_up, w_pool, pool_scale, w_b_up, w_out, w_ffn2_gu, w_ffn2_down, w_ple_proj, w_ple_gate):
    depth = w_in.shape[0]
    past_len = 2048
    bp, tp, d = x_prompt.shape
    bs, ts, _ = x_sample.shape
    xp, xs = x_prompt.reshape(bp * tp, d), x_sample.reshape(bs * ts, d)
    hp_list, bp_list, hs_list, bs_list = [], [], [], []
    for l in range(depth):
        gains = norm_gains[l]
        smalls = (gains, lb_logits, a_gnorm[l].reshape(1, -1), pool_scale[l].reshape(1, -1))
        w1 = (w_ffn1_gu[l], w_ffn1_down[l])
        w_mix = (w_in[l], w_a_up[l], w_pool[l].reshape(-1, POOL_GDIM), w_b_up[l], w_out[l])
        w2 = (w_ffn2_gu[l], w_ffn2_down[l], w_ple_proj[l], w_ple_gate[l])
        ffn1 = functools.partial(_ffn, gains=gains, pre=0, post=1)
        mix = functools.partial(_mixer, smalls=smalls, layer=l, g_pre=2, g_post=3)
        ffn2 = functools.partial(_ffn, gains=gains, pre=4, post=5, ple=(6, 7))

        xp, w1_bf, w_mix_bf = ffn1(xp, weights=w1, own_f32=True, side_weights=w_mix)
        xp, hp, pp, _, w2_bf = mix(xp.reshape(bp, tp, d), weights=w_mix_bf, side_weights=w2, start_pos=0)
        xp, _, _ = ffn2(xp.reshape(bp * tp, d), weights=w2_bf, p=p_prompt[l].reshape(bp * tp, -1))
        xs, _, _ = ffn1(xs, weights=w1_bf)
        xs, hs, ps, _, _ = mix(xs.reshape(bs, ts, d), weights=w_mix_bf, state=state_hgrn[l],
                               hist=state_pool[l], start_pos=past_len)
        xs, _, _ = ffn2(xs.reshape(bs * ts, d), weights=w2_bf, p=p_sample[l].reshape(bs * ts, -1))
        hp_list.append(hp)
        bp_list.append(pp)
        hs_list.append(hs)
        bs_list.append(ps)
    return (xp.reshape(bp, tp, d), xs.reshape(bs, ts, d), jnp.stack(hp_list), jnp.stack(bp_list),
            jnp.stack(hs_list), jnp.stack(bs_list))
```

```python
import functools

import jax
import jax.numpy as jnp
from jax import lax
from jax.experimental import pallas as pl
from jax.experimental.pallas import tpu as pltpu

EPS = 1e-6
HEAD_DIM = 128
POOL_WINDOWS = (2, 4, 8, 16)
POOL_GDIM = 128
POOL_HIST = 16
POOL_PAD = 8
DECAY_BLK = 64
ATT_SPAN = 256
STREAM_TILE = 512
FFN_TILE = 1024
FFN_SUB_ROWS = 512
FF_CHUNK = 256
WEIGHT_STEPS = 16
BF16_SUBLANES = 16
V7X_VMEM_LIMIT_BYTES = 56 * 1024 * 1024

BF16 = jnp.bfloat16
F32 = jnp.float32


def _dot(a, b):
    return jnp.dot(a, b, preferred_element_type=F32)


def _dot_nt(a, b):
    return lax.dot_general(a, b, (((1,), (1,)), ((), ())), preferred_element_type=F32)


def _dot_tn(a, b):
    return lax.dot_general(a, b, (((0,), (0,)), ((), ())), preferred_element_type=F32)


def _rmsnorm(x, g):
    ms = jnp.mean(x * x, axis=-1, keepdims=True)
    return x * lax.rsqrt(ms + EPS) * g


def _silu(x):
    return x * jax.nn.sigmoid(x)


def _resident(shape):
    zeros = (0,) * len(shape)
    return pl.BlockSpec(shape, lambda *_: zeros, pipeline_mode=pl.Buffered(1))


def _stream_kernel(*refs, body, n_tiles, n_acts, n_smalls, n_tail, n_own, n_out, side_steps, own_f32):
    refs = list(refs)
    take = lambda n: [refs.pop(0) for _ in range(n)]
    acts, smalls, tail_in = take(n_acts), take(n_smalls), take(n_tail)
    own_in, side_in = take(n_own), take(len(side_steps))
    outs, tail_out, side_out = take(n_out), take(n_out if n_tail else 0), take(len(side_steps))
    resident = take(n_own) if own_f32 else own_in
    scratch = refs
    first = WEIGHT_STEPS if own_f32 else 0
    i = pl.program_id(0)
    t = i - first

    if own_f32:
        @pl.when(i < first)
        def _():
            for chunk, full in zip(own_in, resident):
                rows = chunk.shape[0]
                full[pl.ds(pl.multiple_of(i * rows, rows), rows), :] = chunk[...].astype(BF16)

    @pl.when((t >= 0) & (t < n_tiles))
    def _():
        for chunk, chunk_out, steps in zip(side_in, side_out, side_steps):
            @pl.when(t < steps)
            def _():
                chunk_out[...] = chunk[...].astype(BF16)
        body(*acts, *smalls, *resident, *outs, *scratch, step=t)

    if n_tail:
        @pl.when(t == n_tiles)
        def _():
            body(*tail_in, *smalls, *resident, *tail_out, *scratch, step=0)


def _chunk_steps(rows, max_steps):
    for steps in range(max_steps, 0, -1):
        if rows % (steps * BF16_SUBLANES) == 0:
            return steps
    raise ValueError(f"{rows} rows do not split into bf16 tiles")


def _stream_call(body, name, n_tiles, acts, act_specs, smalls, weights, own_f32, side_weights,
                 out_specs, out_shape, scratch, tail=(), tail_out_shape=()):
    first = WEIGHT_STEPS if own_f32 else 0
    ts = lambda i: jnp.clip(i - first, 0, n_tiles - 1)
    whole = lambda a: pl.BlockSpec(a.shape, lambda i: (0,) * len(a.shape))

    def chunked(w, steps, step_of):
        rows, cols = w.shape
        return pl.BlockSpec((rows // steps, cols), lambda i: (jnp.minimum(step_of(i), steps - 1), 0))

    if own_f32:
        assert all(_chunk_steps(w.shape[0], WEIGHT_STEPS) == WEIGHT_STEPS for w in weights)
        own_specs = [chunked(w, WEIGHT_STEPS, lambda i: i) for w in weights]
    else:
        own_specs = [_resident(w.shape) for w in weights]
    side_steps = tuple(_chunk_steps(w.shape[0], n_tiles) for w in side_weights)
    side_specs = [chunked(w, s, ts) for w, s in zip(side_weights, side_steps)]
    side_shape = [jax.ShapeDtypeStruct(w.shape, BF16) for w in side_weights]
    result = pl.pallas_call(
        functools.partial(_stream_kernel, body=body, n_tiles=n_tiles, n_acts=len(acts), n_smalls=len(smalls),
                          n_tail=len(tail), n_own=len(weights), n_out=len(out_shape),
                          side_steps=side_steps, own_f32=own_f32),
        grid=(first + n_tiles + (1 if tail else 0),),
        in_specs=([spec(ts) for spec in act_specs] + [_resident(a.shape) for a in smalls]
                  + [_resident(a.shape) for a in tail] + own_specs + side_specs),
        out_specs=[spec(ts) for spec in out_specs] + [whole(s) for s in tail_out_shape] + side_specs,
        out_shape=list(out_shape) + list(tail_out_shape) + side_shape,
        scratch_shapes=([pltpu.VMEM(w.shape, BF16) for w in weights] if own_f32 else []) + list(scratch),
        compiler_params=pltpu.CompilerParams(
            dimension_semantics=("arbitrary",), vmem_limit_bytes=V7X_VMEM_LIMIT_BYTES),
        name=name,
    )(*acts, *smalls, *tail, *weights, *side_weights)
    result = list(result)
    outs = [result.pop(0) for _ in out_shape]
    tail_outs = [result.pop(0) for _ in tail_out_shape]
    return outs, tail_outs, result


def _ffn_body(*refs, sub_rows, pre, post, ple, step=None):
    if ple is None:
        x_ref, gains_ref, wgu_ref, wd_ref, o_ref, act_ref = refs
    else:
        x_ref, p_ref, gains_ref, wgu_ref, wd_ref, wpp_ref, wpg_ref, o_ref, act_ref = refs
    d_ff = wd_ref.shape[0]
    n_rows = x_ref.shape[0]
    sub_rows = min(sub_rows, n_rows)
    assert n_rows % sub_rows == 0
    for r0 in range(0, n_rows, sub_rows):
        rows = slice(r0, r0 + sub_rows)
        x = x_ref[rows, :]
        h = _rmsnorm(x, gains_ref[pre:pre + 1, :]).astype(BF16)
        for lo in range(0, d_ff, FF_CHUNK):
            a = _dot(h, wgu_ref[:, lo:lo + FF_CHUNK])
            b = _dot(h, wgu_ref[:, d_ff + lo:d_ff + lo + FF_CHUNK])
            act_ref[rows, lo:lo + FF_CHUNK] = (_silu(a) * b).astype(BF16)
        y = _dot(act_ref[rows, :], wd_ref[...])
        x = x + 0.5 * _rmsnorm(y, gains_ref[post:post + 1, :])
        if ple is not None:
            g_pre, g_post = ple
            e = _dot(p_ref[rows, :].astype(BF16), wpp_ref[...])
            gate = jax.nn.sigmoid(_dot(_rmsnorm(x, gains_ref[g_pre:g_pre + 1, :]).astype(BF16), wpg_ref[...]))
            x = x + _rmsnorm(gate * e, gains_ref[g_post:g_post + 1, :])
        o_ref[rows, :] = x


def _ffn(x, x_tail, gains, weights, *, pre, post, p=None, p_tail=None, ple=None, own_f32=False,
         side_weights=()):
    n, d = x.shape
    d_ff = weights[1].shape[0]
    tm = min(FFN_TILE, n)
    assert n % tm == 0 and x_tail.shape[0] <= tm and d_ff % FF_CHUNK == 0
    body = functools.partial(_ffn_body, sub_rows=FFN_SUB_ROWS, pre=pre, post=post, ple=ple)
    acts, tail = ([x], [x_tail]) if ple is None else ([x, p], [x_tail, p_tail])
    row = lambda width: lambda ts: pl.BlockSpec((tm, width), lambda i: (ts(i), 0))
    (y,), (y_tail,), side_bf = _stream_call(
        body, "ffn" if ple is None else "ffn_ple", n // tm,
        acts, [row(a.shape[1]) for a in acts], [gains], list(weights), own_f32, list(side_weights),
        [row(d)], [jax.ShapeDtypeStruct((n, d), F32)], [pltpu.VMEM((tm, d_ff), BF16)],
        tail=tail, tail_out_shape=[jax.ShapeDtypeStruct(x_tail.shape, F32)])
    return y, y_tail, side_bf


def _mixer_body(*refs, layer, n_heads, tile, span, chunk, blk, seq_rows, carry, start_pos, g_pre, g_post,
                tiles_per_seq=1, step=0):
    if carry:
        (x_ref, gains_ref, lbl_ref, agn_ref, pscale_ref, win_ref, waup_ref, wpool_ref, wbup_ref, wout_ref,
         o_ref, s_out_ref, pool_out_ref,
         st_scr, hist_scr, lvl_scr, gated_scr, mixed_scr) = refs
        step = step % tiles_per_seq
    else:
        (x_ref, s0_ref, hist0_ref, gains_ref, lbl_ref, agn_ref, pscale_ref,
         win_ref, waup_ref, wpool_ref, wbup_ref, wout_ref,
         o_ref, s_out_ref, pool_out_ref,
         st_scr, lvl_scr, gated_scr, mixed_scr) = refs
    a_width = n_heads * HEAD_DIM
    b_width = len(POOL_WINDOWS) * POOL_GDIM
    d_model = x_ref.shape[-1]
    n_seq = tile // seq_rows
    n_chunks = tile // chunk
    ext_rows = POOL_PAD + POOL_HIST + seq_rows

    if carry:
        @pl.when(step == 0)
        def _():
            st_scr[...] = jnp.zeros_like(st_scr)
            hist_scr[...] = jnp.zeros_like(hist_scr)
    else:
        for s in range(n_seq):
            for hd in range(n_heads):
                st_scr[s * n_heads + hd] = s0_ref[s, hd].T

    x = x_ref[...]
    h = _rmsnorm(x, gains_ref[g_pre:g_pre + 1, :]).astype(BF16)

    logits = lbl_ref[...]
    ex = jnp.exp(logits - jnp.max(logits, axis=0, keepdims=True))
    lb = jnp.sum(ex[:layer + 1], axis=0, keepdims=True) / jnp.sum(ex, axis=0, keepdims=True)

    r_idx = lax.broadcasted_iota(jnp.int32, (span, span), 0)
    c_idx = lax.broadcasted_iota(jnp.int32, (span, span), 1)
    tri = jnp.where((c_idx <= r_idx) & (c_idx >= (r_idx // blk) * blk), 1.0, 0.0).astype(BF16)
    causal = (c_idx <= r_idx) & (c_idx >= (r_idx // chunk) * chunk)

    heads = [slice(hd * HEAD_DIM, (hd + 1) * HEAD_DIM) for hd in range(n_heads)]
    blocks = [slice(c * chunk, (c + 1) * chunk) for c in range(n_chunks)]
    subs = [slice(b * blk, (b + 1) * blk) for b in range(tile // blk)]
    two_level = chunk == 2 * blk
    assert two_level or chunk == blk
    proj = lambda lo_col, width: _dot(h, win_ref[:, lo_col:lo_col + width])
    fz = proj(a_width, a_width)
    q = proj(0, a_width)
    e = jnp.exp(-jnp.abs(fz))
    r = 1.0 / (1.0 + e)
    kk = (1.0 - lb) * jnp.where(fz >= 0.0, e * r, r)
    logf = jnp.log2(1.0 - kk)
    hi = logf.astype(BF16)
    lo = (logf - hi.astype(F32)).astype(BF16)
    v = proj(2 * a_width, a_width).astype(BF16)
    u = proj(4 * a_width, b_width)
    spans = [slice(s * span, (s + 1) * span) for s in range(tile // span)]
    g_parts = [_dot(tri, hi[sp]) + _dot(tri, lo[sp]) for sp in spans]
    g_cum = jnp.concatenate(g_parts, axis=0) if len(spans) > 1 else g_parts[0]
    og = proj(3 * a_width, a_width)
    ga = proj(4 * a_width + b_width, d_model)
    gb = proj(4 * a_width + b_width + d_model, d_model)
    rows_cat = lambda parts: jnp.concatenate(parts, axis=0) if len(parts) > 1 else parts[0]
    dec = jnp.exp2(g_cum)
    sub_dec = [dec[sb.stop - 1:sb.stop, :] for sb in subs]
    q_loc = _silu(q) * dec
    k_loc = kk * jnp.exp2(-g_cum)
    k_fwd = rows_cat([k_loc[sb] * sub_dec[b] for b, sb in enumerate(subs)])
    if two_level:
        q_chunk = rows_cat([q_loc[sb] * sub_dec[b - 1] if b % 2 else q_loc[sb] for b, sb in enumerate(subs)])
        k_end = rows_cat([k_fwd[sb] if b % 2 else k_fwd[sb] * sub_dec[b + 1] for b, sb in enumerate(subs)])
        d_chunk = [sub_dec[2 * c] * sub_dec[2 * c + 1] for c in range(n_chunks)]
    else:
        q_chunk, k_end, d_chunk = q_loc, k_fwd, sub_dec
    q_loc_bf, k_loc_bf = q_loc.astype(BF16), k_loc.astype(BF16)
    q_chunk_bf, k_end_bf = q_chunk.astype(BF16), k_end.astype(BF16)
    if two_level:
        k_fwd_bf = k_fwd.astype(BF16)
        zero = jnp.zeros((blk, HEAD_DIM), BF16)

        def scores(sp, ls):
            bs = range(sp.start // blk, sp.stop // blk)
            lhs = jnp.concatenate(
                [rows_cat([zero if b % 2 else q_loc_bf[subs[b], ls] for b in bs]),
                 rows_cat([q_loc_bf[subs[b], ls] if b % 2 else zero for b in bs])], axis=1)
            rhs = jnp.concatenate(
                [rows_cat([zero if b % 2 else k_loc_bf[subs[b], ls] for b in bs]),
                 rows_cat([k_loc_bf[subs[b], ls] if b % 2 else k_fwd_bf[subs[b], ls] for b in bs])], axis=1)
            return _dot_nt(lhs, rhs)
    else:
        scores = lambda sp, ls: _dot_nt(q_loc_bf[sp, ls], k_loc_bf[sp, ls])
    att = [[jnp.where(causal, scores(sp, ls), 0.0).astype(BF16) for ls in heads] for sp in spans]
    o_intra = [[_dot(att[s][hd], v[sp, ls]) for hd, ls in enumerate(heads)] for s, sp in enumerate(spans)]
    kv = [[_dot_tn(v[rows, ls], k_end_bf[rows, ls]) for ls in heads] for rows in blocks]
    o_heads = [[] for _ in range(n_heads)]
    state = {}
    for c, rows in enumerate(blocks):
        for hd, ls in enumerate(heads):
            s_idx = hd if carry else (rows.start // seq_rows) * n_heads + hd
            st = state[s_idx] if s_idx in state else st_scr[s_idx]
            s, lo_row = divmod(rows.start, span)
            o_heads[hd].append(o_intra[s][hd][lo_row:lo_row + chunk]
                               + _dot_nt(q_chunk_bf[rows, ls], st.astype(BF16)))
            state[s_idx] = st * d_chunk[c][:, ls] + kv[c][hd]
    for s_idx, st in state.items():
        st_scr[s_idx] = st
    gated_og = _silu(og)
    for hd, ls in enumerate(heads):
        o = jnp.concatenate(o_heads[hd], axis=0) if n_chunks > 1 else o_heads[hd][0]
        o = o * lax.rsqrt(jnp.mean(o * o, axis=-1, keepdims=True) + EPS) * agn_ref[...]
        gated_scr[:, ls] = (o * gated_og[:, ls]).astype(BF16)
    y_a = _dot(gated_scr[...], waup_ref[...])

    row = lax.broadcasted_iota(jnp.int32, (seq_rows, 1), 0)
    first_pos = start_pos + (step * tile if carry else 0)
    n_levels = len(POOL_WINDOWS)
    front = POOL_PAD + POOL_HIST
    for s in range(n_seq):
        base = s * ext_rows
        for k in range(n_levels - 1):
            lvl_scr[k, base:base + POOL_PAD, :] = jnp.zeros((POOL_PAD, b_width), F32)
        if carry:
            lvl_scr[0, base + POOL_PAD:base + front, :] = hist_scr[...]
        else:
            lvl_scr[0, base + POOL_PAD:base + POOL_PAD + 1, :] = jnp.zeros((1, b_width), F32)
            lvl_scr[0, base + POOL_PAD + 1:base + front, :] = hist0_ref[s]
        lvl_scr[0, base + front:base + ext_rows, :] = u[s * seq_rows:(s + 1) * seq_rows]
    for s in range(n_seq):
        base = s * ext_rows
        out_rows = slice(s * seq_rows, (s + 1) * seq_rows)
        for k in range(1, n_levels + 1):
            g, w, shift = k - 1, POOL_WINDOWS[k - 1], 2 ** (k - 1)
            assert w == 2 * shift
            lo_col = g * POOL_GDIM
            cols = slice(lo_col, lo_col + POOL_GDIM)
            lo_row = base + (front if k == n_levels else POOL_PAD)
            summed = (lvl_scr[k - 1, lo_row:base + ext_rows, lo_col:]
                      + lvl_scr[k - 1, lo_row - shift:base + ext_rows - shift, lo_col:])
            if k < n_levels:
                lvl_scr[k, lo_row:base + ext_rows, lo_col:] = summed
                summed = summed[POOL_HIST:, :POOL_GDIM]
            cur = lvl_scr[0, base + front:base + ext_rows, cols]
            cnt = jnp.minimum(w, first_pos + row + 1).astype(F32)
            pooled = summed / cnt - cur
            mixed = _dot(pooled.astype(BF16), wpool_ref[cols, :]) * pscale_ref[:, cols]
            mixed_scr[out_rows, cols] = mixed.astype(BF16)
        last = lvl_scr[0, base + ext_rows - POOL_HIST:base + ext_rows, :]
        if carry:
            hist_scr[...] = last
            pool_out_ref[...] = last
        else:
            pool_out_ref[s] = last
    y_b = _dot(mixed_scr[...], wbup_ref[...])

    m = (jax.nn.sigmoid(ga) * y_a + jax.nn.sigmoid(gb) * y_b).astype(BF16)
    o_ref[...] = x + _rmsnorm(_dot(m, wout_ref[...]), gains_ref[g_post:g_post + 1, :])

    if carry:
        @pl.when(step == tiles_per_seq - 1)
        def _():
            for hd in range(n_heads):
                s_out_ref[hd] = st_scr[hd].T
    else:
        for s in range(n_seq):
            for hd in range(n_heads):
                s_out_ref[s, hd] = st_scr[s * n_heads + hd].T


def _mixer(x, smalls, weights, *, layer, state=None, hist=None, start_pos, g_pre, g_post, side_weights=()):
    bsz, t_len, d = x.shape
    n_heads = weights[1].shape[0] // HEAD_DIM
    b_width = weights[3].shape[0]
    carry = state is None
    if carry:
        tile = min(STREAM_TILE, t_len)
        blk = min(DECAY_BLK, tile)
        chunk = min(2 * blk, tile)
        assert t_len % tile == 0 and tile % chunk == 0 and tile >= POOL_HIST
        seq_rows, n_seq, n_state = tile, 1, n_heads
    else:
        tile = bsz * t_len
        chunk = blk = seq_rows = t_len
        assert t_len % BF16_SUBLANES == 0 and t_len >= POOL_HIST
        n_seq, n_state = bsz, bsz * n_heads
    span = min(ATT_SPAN, tile)
    assert tile % span == 0 and span % chunk == 0
    tiles_per_seq = t_len // tile if carry else 1
    body = functools.partial(_mixer_body, layer=layer, n_heads=n_heads, tile=tile, span=span, chunk=chunk,
                             blk=blk, seq_rows=seq_rows, carry=carry, start_pos=start_pos,
                             g_pre=g_pre, g_post=g_post, tiles_per_seq=tiles_per_seq)
    scratch = [pltpu.VMEM((n_state, HEAD_DIM, HEAD_DIM), F32)]
    if carry:
        scratch.append(pltpu.VMEM((POOL_HIST, b_width), F32))
    scratch += [
        pltpu.VMEM((len(POOL_WINDOWS), n_seq * (POOL_PAD + POOL_HIST + seq_rows), b_width), F32),
        pltpu.VMEM((tile, n_heads * HEAD_DIM), BF16),
        pltpu.VMEM((tile, b_width), BF16),
    ]
    if carry:
        n_tiles = bsz * tiles_per_seq
        seq_tile = lambda ts: lambda i: (ts(i) // tiles_per_seq, ts(i) % tiles_per_seq, 0)
        per_seq = lambda ts: lambda i: (ts(i) // tiles_per_seq, 0, 0)
        acts = [x]
        act_specs = [lambda ts: pl.BlockSpec((None, tile, d), seq_tile(ts))]
        out_specs = [
            lambda ts: pl.BlockSpec((None, tile, d), seq_tile(ts)),
            lambda ts: pl.BlockSpec((None, n_heads, HEAD_DIM, HEAD_DIM), lambda i: per_seq(ts)(i) + (0,)),
            lambda ts: pl.BlockSpec((None, POOL_HIST, b_width), per_seq(ts)),
        ]
        out_shape = [
            jax.ShapeDtypeStruct((bsz, t_len, d), F32),
            jax.ShapeDtypeStruct((bsz, n_heads, HEAD_DIM, HEAD_DIM), F32),
            jax.ShapeDtypeStruct((bsz, POOL_HIST, b_width), F32),
        ]
    else:
        n_tiles = 1
        whole = lambda shape: lambda ts: pl.BlockSpec(shape, lambda i: (0,) * len(shape))
        acts = [x.reshape(tile, d), state, hist]
        act_specs = [whole((tile, d)), lambda ts: _resident(state.shape), lambda ts: _resident(hist.shape)]
        out_shape = [
            jax.ShapeDtypeStruct((tile, d), F32),
            jax.ShapeDtypeStruct(state.shape, F32),
            jax.ShapeDtypeStruct((bsz, POOL_HIST, b_width), F32),
        ]
        out_specs = [whole(s.shape) for s in out_shape]
    (y, s_new, pool_new), _, side_bf = _stream_call(
        body, "mixer", n_tiles, acts, act_specs, list(smalls), list(weights), False, list(side_weights),
        out_specs, out_shape, scratch)
    return y.reshape(bsz, t_len, d), s_new, pool_new[:, 1:, :], side_bf


def kernel(x_prompt, x_sample, p_prompt, p_sample, state_hgrn, state_pool, norm_gains, lb_logits, w_ffn1_gu, w_ffn1_down, w_in, a_gnorm, w_a_up, w_pool, pool_scale, w_b_up, w_out, w_ffn2_gu, w_ffn2_down, w_ple_proj, w_ple_gate):
    depth = w_in.shape[0]
    past_len = 2048
    bp, tp, d = x_prompt.shape
    bs, ts, _ = x_sample.shape
    xp, xs = x_prompt.reshape(bp * tp, d), x_sample.reshape(bs * ts, d)
    hp_list, bp_list, hs_list, bs_list = [], [], [], []
    for l in range(depth):
        gains = norm_gains[l]
        smalls = (gains, lb_logits, a_gnorm[l].reshape(1, -1), pool_scale[l].reshape(1, -1))
        w1 = (w_ffn1_gu[l], w_ffn1_down[l])
        w_mix = (w_in[l], w_a_up[l], w_pool[l].reshape(-1, POOL_GDIM), w_b_up[l], w_out[l])
        w2 = (w_ffn2_gu[l], w_ffn2_down[l], w_ple_proj[l], w_ple_gate[l])
        ffn1 = functools.partial(_ffn, gains=gains, pre=0, post=1)
        mix = functools.partial(_mixer, smalls=smalls, layer=l, g_pre=2, g_post=3)
        ffn2 = functools.partial(_ffn, gains=gains, pre=4, post=5, ple=(6, 7))

        xp, xs, w_mix_bf = ffn1(xp, xs, weights=w1, own_f32=True, side_weights=w_mix)
        xp, hp, pp, w2_bf = mix(xp.reshape(bp, tp, d), weights=w_mix_bf, side_weights=w2, start_pos=0)
        xs, hs, ps, _ = mix(xs.reshape(bs, ts, d), weights=w_mix_bf, state=state_hgrn[l],
                            hist=state_pool[l], start_pos=past_len)
        xp, xs, _ = ffn2(xp.reshape(bp * tp, d), xs.reshape(bs * ts, d), weights=w2_bf,
                         p=p_prompt[l].reshape(bp * tp, -1), p_tail=p_sample[l].reshape(bs * ts, -1))
        hp_list.append(hp)
        bp_list.append(pp)
        hs_list.append(hs)
        bs_list.append(ps)
    return (xp.reshape(bp, tp, d), xs.reshape(bs, ts, d), jnp.stack(hp_list), jnp.stack(bp_list),
            jnp.stack(hs_list), jnp.stack(bs_list))
```

```python
import functools

import jax
import jax.numpy as jnp
from jax import lax
from jax.experimental import pallas as pl
from jax.experimental.pallas import tpu as pltpu

EPS = 1e-6
HEAD_DIM = 128
POOL_WINDOWS = (2, 4, 8, 16)
POOL_GDIM = 128
POOL_HIST = 16
POOL_PAD = 8
DECAY_BLK = 64
ATT_SPAN = 256
STREAM_TILE = 512
FFN_TILE = 1024
FFN_SUB_ROWS = 512
FF_CHUNK = 256
WEIGHT_STEPS = 16
BF16_SUBLANES = 16
V7X_VMEM_LIMIT_BYTES = 56 * 1024 * 1024

BF16 = jnp.bfloat16
F32 = jnp.float32


def _dot(a, b):
    return jnp.dot(a, b, preferred_element_type=F32)


def _dot_nt(a, b):
    return lax.dot_general(a, b, (((1,), (1,)), ((), ())), preferred_element_type=F32)


def _dot_tn(a, b):
    return lax.dot_general(a, b, (((0,), (0,)), ((), ())), preferred_element_type=F32)


def _rmsnorm(x, g):
    ms = jnp.mean(x * x, axis=-1, keepdims=True)
    return x * lax.rsqrt(ms + EPS) * g


def _silu(x):
    return x * jax.nn.sigmoid(x)


def _resident(shape):
    zeros = (0,) * len(shape)
    return pl.BlockSpec(shape, lambda *_: zeros, pipeline_mode=pl.Buffered(1))


def _stream_kernel(*refs, body, n_tiles, n_acts, n_smalls, n_tail, n_own, n_out, n_side, own_f32):
    refs = list(refs)
    take = lambda n: [refs.pop(0) for _ in range(n)]
    acts, smalls, tail_in = take(n_acts), take(n_smalls), take(n_tail)
    own_in, side_in = take(n_own), take(n_side)
    outs, tail_out, side_out = take(n_out), take(n_out if n_tail else 0), take(n_side)
    resident = take(n_own) if own_f32 else own_in
    scratch = refs
    first = WEIGHT_STEPS if own_f32 else 0
    i = pl.program_id(0)
    t = i - first

    if own_f32:
        @pl.when(i < first)
        def _():
            for chunk, full in zip(own_in, resident):
                rows = chunk.shape[0]
                full[pl.ds(pl.multiple_of(i * rows, rows), rows), :] = chunk[...].astype(BF16)

    @pl.when((t >= 0) & (t < n_tiles))
    def _():
        for chunk, chunk_out in zip(side_in, side_out):
            chunk_out[...] = chunk[...].astype(BF16)
        body(*acts, *smalls, *resident, *outs, *scratch, step=t)

    if n_tail:
        @pl.when(t == n_tiles)
        def _():
            body(*tail_in, *smalls, *resident, *tail_out, *scratch, step=0)


def _chunk_steps(rows, max_steps):
    for steps in range(max_steps, 0, -1):
        if rows % (steps * BF16_SUBLANES) == 0:
            return steps
    raise ValueError(f"{rows} rows do not split into bf16 tiles")


def _stream_call(body, name, n_tiles, acts, act_specs, smalls, weights, own_f32, side_weights,
                 out_specs, out_shape, scratch, tail=(), tail_out_shape=()):
    first = WEIGHT_STEPS if own_f32 else 0
    ts = lambda i: jnp.clip(i - first, 0, n_tiles - 1)
    whole = lambda a: pl.BlockSpec(a.shape, lambda i: (0,) * len(a.shape))

    def chunked(w, steps, step_of):
        rows, cols = w.shape
        return pl.BlockSpec((rows // steps, cols), lambda i: (jnp.minimum(step_of(i), steps - 1), 0))

    if own_f32:
        assert all(_chunk_steps(w.shape[0], WEIGHT_STEPS) == WEIGHT_STEPS for w in weights)
        own_specs = [chunked(w, WEIGHT_STEPS, lambda i: i) for w in weights]
    else:
        own_specs = [_resident(w.shape) for w in weights]
    side_steps = tuple(_chunk_steps(w.shape[0], n_tiles) for w in side_weights)
    side_specs = [chunked(w, s, ts) for w, s in zip(side_weights, side_steps)]
    side_shape = [jax.ShapeDtypeStruct(w.shape, BF16) for w in side_weights]
    result = pl.pallas_call(
        functools.partial(_stream_kernel, body=body, n_tiles=n_tiles, n_acts=len(acts), n_smalls=len(smalls),
                          n_tail=len(tail), n_own=len(weights), n_out=len(out_shape),
                          n_side=len(side_weights), own_f32=own_f32),
        grid=(first + n_tiles + (1 if tail else 0),),
        in_specs=([spec(ts) for spec in act_specs] + [_resident(a.shape) for a in smalls]
                  + [_resident(a.shape) for a in tail] + own_specs + side_specs),
        out_specs=[spec(ts) for spec in out_specs] + [whole(s) for s in tail_out_shape] + side_specs,
        out_shape=list(out_shape) + list(tail_out_shape) + side_shape,
        scratch_shapes=([pltpu.VMEM(w.shape, BF16) for w in weights] if own_f32 else []) + list(scratch),
        compiler_params=pltpu.CompilerParams(
            dimension_semantics=("arbitrary",), vmem_limit_bytes=V7X_VMEM_LIMIT_BYTES),
        name=name,
    )(*acts, *smalls, *tail, *weights, *side_weights)
    result = list(result)
    outs = [result.pop(0) for _ in out_shape]
    tail_outs = [result.pop(0) for _ in tail_out_shape]
    return outs, tail_outs, result


def _ffn_body(*refs, sub_rows, pre, post, ple, step):
    if ple is None:
        x_ref, gains_ref, wgu_ref, wd_ref, o_ref, act_ref = refs
    else:
        x_ref, p_ref, gains_ref, wgu_ref, wd_ref, wpp_ref, wpg_ref, o_ref, act_ref = refs
    d_ff = wd_ref.shape[0]
    n_rows = x_ref.shape[0]
    sub_rows = min(sub_rows, n_rows)
    assert n_rows % sub_rows == 0
    for r0 in range(0, n_rows, sub_rows):
        rows = slice(r0, r0 + sub_rows)
        x = x_ref[rows, :]
        h = _rmsnorm(x, gains_ref[pre:pre + 1, :]).astype(BF16)
        for lo in range(0, d_ff, FF_CHUNK):
            a = _dot(h, wgu_ref[:, lo:lo + FF_CHUNK])
            b = _dot(h, wgu_ref[:, d_ff + lo:d_ff + lo + FF_CHUNK])
            act_ref[rows, lo:lo + FF_CHUNK] = (_silu(a) * b).astype(BF16)
        y = _dot(act_ref[rows, :], wd_ref[...])
        x = x + 0.5 * _rmsnorm(y, gains_ref[post:post + 1, :])
        if ple is not None:
            g_pre, g_post = ple
            e = _dot(p_ref[rows, :].astype(BF16), wpp_ref[...])
            gate = jax.nn.sigmoid(_dot(_rmsnorm(x, gains_ref[g_pre:g_pre + 1, :]).astype(BF16), wpg_ref[...]))
            x = x + _rmsnorm(gate * e, gains_ref[g_post:g_post + 1, :])
        o_ref[rows, :] = x


def _ffn(x, x_tail, gains, weights, *, pre, post, p=None, p_tail=None, ple=None, own_f32=False,
         side_weights=()):
    n, d = x.shape
    d_ff = weights[1].shape[0]
    tm = min(FFN_TILE, n)
    assert n % tm == 0 and x_tail.shape[0] <= tm and d_ff % FF_CHUNK == 0
    body = functools.partial(_ffn_body, sub_rows=FFN_SUB_ROWS, pre=pre, post=post, ple=ple)
    acts, tail = ([x], [x_tail]) if ple is None else ([x, p], [x_tail, p_tail])
    row = lambda width: lambda ts: pl.BlockSpec((tm, width), lambda i: (ts(i), 0))
    (y,), (y_tail,), side_bf = _stream_call(
        body, "ffn" if ple is None else "ffn_ple", n // tm,
        acts, [row(a.shape[1]) for a in acts], [gains], list(weights), own_f32, list(side_weights),
        [row(d)], [jax.ShapeDtypeStruct((n, d), F32)], [pltpu.VMEM((tm, d_ff), BF16)],
        tail=tail, tail_out_shape=[jax.ShapeDtypeStruct(x_tail.shape, F32)])
    return y, y_tail, side_bf


def _mixer_body(*refs, layer, n_heads, tile, span, chunk, blk, seq_rows, carry, start_pos, g_pre, g_post,
                tiles_per_seq, step):
    if carry:
        (x_ref, gains_ref, lbl_ref, agn_ref, pscale_ref, win_ref, waup_ref, wpool_ref, wbup_ref, wout_ref,
         o_ref, s_out_ref, pool_out_ref,
         st_scr, hist_scr, lvl_scr, gated_scr, mixed_scr) = refs
        step = step % tiles_per_seq
    else:
        (x_ref, s0_ref, hist0_ref, gains_ref, lbl_ref, agn_ref, pscale_ref,
         win_ref, waup_ref, wpool_ref, wbup_ref, wout_ref,
         o_ref, s_out_ref, pool_out_ref,
         st_scr, lvl_scr, gated_scr, mixed_scr) = refs
    a_width = n_heads * HEAD_DIM
    b_width = len(POOL_WINDOWS) * POOL_GDIM
    d_model = x_ref.shape[-1]
    n_seq = tile // seq_rows
    n_chunks = tile // chunk
    ext_rows = POOL_PAD + POOL_HIST + seq_rows

    if carry:
        @pl.when(step == 0)
        def _():
            st_scr[...] = jnp.zeros_like(st_scr)
            hist_scr[...] = jnp.zeros_like(hist_scr)
    else:
        for s in range(n_seq):
            for hd in range(n_heads):
                st_scr[s * n_heads + hd] = s0_ref[s, hd].T

    x = x_ref[...]
    h = _rmsnorm(x, gains_ref[g_pre:g_pre + 1, :]).astype(BF16)

    logits = lbl_ref[...]
    ex = jnp.exp(logits - jnp.max(logits, axis=0, keepdims=True))
    lb = jnp.sum(ex[:layer + 1], axis=0, keepdims=True) / jnp.sum(ex, axis=0, keepdims=True)

    r_idx = lax.broadcasted_iota(jnp.int32, (span, span), 0)
    c_idx = lax.broadcasted_iota(jnp.int32, (span, span), 1)
    tri = jnp.where((c_idx <= r_idx) & (c_idx >= (r_idx // blk) * blk), 1.0, 0.0).astype(BF16)
    causal = (c_idx <= r_idx) & (c_idx >= (r_idx // chunk) * chunk)

    heads = [slice(hd * HEAD_DIM, (hd + 1) * HEAD_DIM) for hd in range(n_heads)]
    blocks = [slice(c * chunk, (c + 1) * chunk) for c in range(n_chunks)]
    subs = [slice(b * blk, (b + 1) * blk) for b in range(tile // blk)]
    two_level = chunk == 2 * blk
    assert two_level or chunk == blk
    proj = lambda lo_col, width: _dot(h, win_ref[:, lo_col:lo_col + width])
    fz = proj(a_width, a_width)
    e = jnp.exp(-jnp.abs(fz))
    r = 1.0 / (1.0 + e)
    kk = (1.0 - lb) * jnp.where(fz >= 0.0, e * r, r)
    logf = jnp.log2(1.0 - kk)
    hi = logf.astype(BF16)
    lo = (logf - hi.astype(F32)).astype(BF16)
    q_act = _silu(proj(0, a_width))
    v = proj(2 * a_width, a_width).astype(BF16)
    u = proj(4 * a_width, b_width)
    spans = [slice(s * span, (s + 1) * span) for s in range(tile // span)]
    g_parts = [_dot(tri, hi[sp]) + _dot(tri, lo[sp]) for sp in spans]
    g_cum = jnp.concatenate(g_parts, axis=0) if len(spans) > 1 else g_parts[0]
    og = proj(3 * a_width, a_width)
    ga = proj(4 * a_width + b_width, d_model)
    gb = proj(4 * a_width + b_width + d_model, d_model)
    rows_cat = lambda parts: jnp.concatenate(parts, axis=0) if len(parts) > 1 else parts[0]
    per_blk = lambda x, factors: rows_cat([x[sb] if f is None else x[sb] * f for sb, f in zip(subs, factors)])
    mid = [g_cum[sb.start + blk // 2 - 1:sb.start + blk // 2, :] for sb in subs]
    end = [g_cum[sb.stop - 1:sb.stop, :] for sb in subs]
    head = [jnp.exp2(m) for m in mid]
    rest = [jnp.exp2(e - m) for e, m in zip(end, mid)]
    whole = [jnp.exp2(e) for e in end]
    g_rel = rows_cat([g_cum[sb] - m for sb, m in zip(subs, mid)])
    q_loc = q_act * jnp.exp2(g_rel)
    k_loc = kk * jnp.exp2(-g_rel)
    odd = lambda b: b % 2 == 1
    n_blks = len(subs)
    if two_level:
        q_chunk = per_blk(q_loc, [head[b] * whole[b - 1] if odd(b) else head[b] for b in range(n_blks)])
        k_end = per_blk(k_loc, [rest[b] if odd(b) else rest[b] * whole[b + 1] for b in range(n_blks)])
        d_chunk = [whole[2 * c] * whole[2 * c + 1] for c in range(n_chunks)]
    else:
        q_chunk, k_end, d_chunk = per_blk(q_loc, head), per_blk(k_loc, rest), whole
    q_loc_bf, k_loc_bf = q_loc.astype(BF16), k_loc.astype(BF16)
    q_chunk_bf, k_end_bf = q_chunk.astype(BF16), k_end.astype(BF16)
    if two_level:
        k_next_bf = per_blk(k_loc, [None if odd(b) else rest[b] * head[b + 1]
                                    for b in range(n_blks)]).astype(BF16)
        zero = jnp.zeros((blk, HEAD_DIM), BF16)

        def scores(sp, ls):
            bs = range(sp.start // blk, sp.stop // blk)
            lhs = jnp.concatenate(
                [rows_cat([zero if odd(b) else q_loc_bf[subs[b], ls] for b in bs]),
                 rows_cat([q_loc_bf[subs[b], ls] if odd(b) else zero for b in bs])], axis=1)
            rhs = jnp.concatenate(
                [rows_cat([zero if odd(b) else k_loc_bf[subs[b], ls] for b in bs]),
                 rows_cat([k_next_bf[subs[b], ls] for b in bs])], axis=1)
            return _dot_nt(lhs, rhs)
    else:
        scores = lambda sp, ls: _dot_nt(q_loc_bf[sp, ls], k_loc_bf[sp, ls])
    att = [[jnp.where(causal, scores(sp, ls), 0.0).astype(BF16) for ls in heads] for sp in spans]
    o_intra = [[_dot(att[s][hd], v[sp, ls]) for hd, ls in enumerate(heads)] for s, sp in enumerate(spans)]
    kv = [[_dot_tn(v[rows, ls], k_end_bf[rows, ls]) for ls in heads] for rows in blocks]
    o_heads = [[] for _ in range(n_heads)]
    state = {}
    for c, rows in enumerate(blocks):
        for hd, ls in enumerate(heads):
            s_idx = hd if carry else (rows.start // seq_rows) * n_heads + hd
            st = state[s_idx] if s_idx in state else st_scr[s_idx]
            s, lo_row = divmod(rows.start, span)
            o_heads[hd].append(o_intra[s][hd][lo_row:lo_row + chunk]
                               + _dot_nt(q_chunk_bf[rows, ls], st.astype(BF16)))
            state[s_idx] = st * d_chunk[c][:, ls] + kv[c][hd]
    for s_idx, st in state.items():
        st_scr[s_idx] = st
    gated_og = _silu(og)
    for hd, ls in enumerate(heads):
        o = jnp.concatenate(o_heads[hd], axis=0) if n_chunks > 1 else o_heads[hd][0]
        o = o * lax.rsqrt(jnp.mean(o * o, axis=-1, keepdims=True) + EPS) * agn_ref[...]
        gated_scr[:, ls] = (o * gated_og[:, ls]).astype(BF16)
    y_a = _dot(gated_scr[...], waup_ref[...])

    row = lax.broadcasted_iota(jnp.int32, (seq_rows, 1), 0)
    first_pos = start_pos + (step * tile if carry else 0)
    n_levels = len(POOL_WINDOWS)
    front = POOL_PAD + POOL_HIST
    for s in range(n_seq):
        base = s * ext_rows
        for k in range(n_levels - 1):
            lvl_scr[k, base:base + POOL_PAD, :] = jnp.zeros((POOL_PAD, b_width), F32)
        if carry:
            lvl_scr[0, base + POOL_PAD:base + front, :] = hist_scr[...]
        else:
            lvl_scr[0, base + POOL_PAD:base + POOL_PAD + 1, :] = jnp.zeros((1, b_width), F32)
            lvl_scr[0, base + POOL_PAD + 1:base + front, :] = hist0_ref[s]
        lvl_scr[0, base + front:base + ext_rows, :] = u[s * seq_rows:(s + 1) * seq_rows]
    for s in range(n_seq):
        base = s * ext_rows
        out_rows = slice(s * seq_rows, (s + 1) * seq_rows)
        for k in range(1, n_levels + 1):
            g, w, shift = k - 1, POOL_WINDOWS[k - 1], 2 ** (k - 1)
            assert w == 2 * shift
            lo_col = g * POOL_GDIM
            cols = slice(lo_col, lo_col + POOL_GDIM)
            lo_row = base + (front if k == n_levels else POOL_PAD)
            summed = (lvl_scr[k - 1, lo_row:base + ext_rows, lo_col:]
                      + lvl_scr[k - 1, lo_row - shift:base + ext_rows - shift, lo_col:])
            if k < n_levels:
                lvl_scr[k, lo_row:base + ext_rows, lo_col:] = summed
                summed = summed[POOL_HIST:, :POOL_GDIM]
            cur = lvl_scr[0, base + front:base + ext_rows, cols]
            cnt = jnp.minimum(w, first_pos + row + 1).astype(F32)
            pooled = summed / cnt - cur
            mixed = _dot(pooled.astype(BF16), wpool_ref[cols, :]) * pscale_ref[:, cols]
            mixed_scr[out_rows, cols] = mixed.astype(BF16)
        last = lvl_scr[0, base + ext_rows - POOL_HIST:base + ext_rows, :]
        if carry:
            hist_scr[...] = last
            pool_out_ref[...] = last
        else:
            pool_out_ref[s] = last
    y_b = _dot(mixed_scr[...], wbup_ref[...])

    m = (jax.nn.sigmoid(ga) * y_a + jax.nn.sigmoid(gb) * y_b).astype(BF16)
    o_ref[...] = x + _rmsnorm(_dot(m, wout_ref[...]), gains_ref[g_post:g_post + 1, :])

    if carry:
        @pl.when(step == tiles_per_seq - 1)
        def _():
            for hd in range(n_heads):
                s_out_ref[hd] = st_scr[hd].T
    else:
        for s in range(n_seq):
            for hd in range(n_heads):
                s_out_ref[s, hd] = st_scr[s * n_heads + hd].T


def _mixer(x, smalls, weights, *, layer, state=None, hist=None, start_pos, g_pre, g_post, side_weights=()):
    bsz, t_len, d = x.shape
    n_heads = weights[1].shape[0] // HEAD_DIM
    b_width = weights[3].shape[0]
    carry = state is None
    if carry:
        tile = min(STREAM_TILE, t_len)
        blk = min(DECAY_BLK, tile)
        chunk = min(2 * blk, tile)
        assert t_len % tile == 0 and tile % chunk == 0 and tile >= POOL_HIST
        seq_rows, n_seq, n_state = tile, 1, n_heads
    else:
        tile = bsz * t_len
        chunk = blk = seq_rows = t_len
        assert t_len % BF16_SUBLANES == 0 and t_len >= POOL_HIST
        n_seq, n_state = bsz, bsz * n_heads
    span = min(ATT_SPAN, tile)
    assert tile % span == 0 and span % chunk == 0
    tiles_per_seq = t_len // tile if carry else 1
    body = functools.partial(_mixer_body, layer=layer, n_heads=n_heads, tile=tile, span=span, chunk=chunk,
                             blk=blk, seq_rows=seq_rows, carry=carry, start_pos=start_pos,
                             g_pre=g_pre, g_post=g_post, tiles_per_seq=tiles_per_seq)
    scratch = [pltpu.VMEM((n_state, HEAD_DIM, HEAD_DIM), F32)]
    if carry:
        scratch.append(pltpu.VMEM((POOL_HIST, b_width), F32))
    scratch += [
        pltpu.VMEM((len(POOL_WINDOWS), n_seq * (POOL_PAD + POOL_HIST + seq_rows), b_width), F32),
        pltpu.VMEM((tile, n_heads * HEAD_DIM), BF16),
        pltpu.VMEM((tile, b_width), BF16),
    ]
    if carry:
        n_tiles = bsz * tiles_per_seq
        seq_tile = lambda ts: lambda i: (ts(i) // tiles_per_seq, ts(i) % tiles_per_seq, 0)
        per_seq = lambda ts: lambda i: (ts(i) // tiles_per_seq, 0, 0)
        acts = [x]
        act_specs = [lambda ts: pl.BlockSpec((None, tile, d), seq_tile(ts))]
        out_specs = [
            lambda ts: pl.BlockSpec((None, tile, d), seq_tile(ts)),
            lambda ts: pl.BlockSpec((None, n_heads, HEAD_DIM, HEAD_DIM), lambda i: per_seq(ts)(i) + (0,)),
            lambda ts: pl.BlockSpec((None, POOL_HIST, b_width), per_seq(ts)),
        ]
        out_shape = [
            jax.ShapeDtypeStruct((bsz, t_len, d), F32),
            jax.ShapeDtypeStruct((bsz, n_heads, HEAD_DIM, HEAD_DIM), F32),
            jax.ShapeDtypeStruct((bsz, POOL_HIST, b_width), F32),
        ]
    else:
        n_tiles = 1
        whole = lambda shape: lambda ts: pl.BlockSpec(shape, lambda i: (0,) * len(shape))
        acts = [x.reshape(tile, d), state, hist]
        act_specs = [whole((tile, d)), lambda ts: _resident(state.shape), lambda ts: _resident(hist.shape)]
        out_shape = [
            jax.ShapeDtypeStruct((tile, d), F32),
            jax.ShapeDtypeStruct(state.shape, F32),
            jax.ShapeDtypeStruct((bsz, POOL_HIST, b_width), F32),
        ]
        out_specs = [whole(s.shape) for s in out_shape]
    (y, s_new, pool_new), _, side_bf = _stream_call(
        body, "mixer", n_tiles, acts, act_specs, list(smalls), list(weights), False, list(side_weights),
        out_specs, out_shape, scratch)
    return y.reshape(bsz, t_len, d), s_new, pool_new[:, 1:, :], side_bf


def kernel(x_prompt, x_sample, p_prompt, p_sample, state_hgrn, state_pool, norm_gains, lb_logits, w_ffn1_gu, w_ffn1_down, w_in, a_gnorm, w_a_up, w_pool, pool_scale, w_b_up, w_out, w_ffn2_gu, w_ffn2_down, w_ple_proj, w_ple_gate):
    depth = w_in.shape[0]
    past_len = 2048
    bp, tp, d = x_prompt.shape
    bs, ts, _ = x_sample.shape
    xp, xs = x_prompt.reshape(bp * tp, d), x_sample.reshape(bs * ts, d)
    hp_list, bp_list, hs_list, bs_list = [], [], [], []
    for l in range(depth):
        gains = norm_gains[l]
        smalls = (gains, lb_logits, a_gnorm[l].reshape(1, -1), pool_scale[l].reshape(1, -1))
        w1 = (w_ffn1_gu[l], w_ffn1_down[l])
        w_mix = (w_in[l], w_a_up[l], w_pool[l].reshape(-1, POOL_GDIM), w_b_up[l], w_out[l])
        w2 = (w_ffn2_gu[l], w_ffn2_down[l], w_ple_proj[l], w_ple_gate[l])
        ffn1 = functools.partial(_ffn, gains=gains, pre=0, post=1)
        mix = functools.partial(_mixer, smalls=smalls, layer=l, g_pre=2, g_post=3)
        ffn2 = functools.partial(_ffn, gains=gains, pre=4, post=5, ple=(6, 7))

        xp, xs, w_mix_bf = ffn1(xp, xs, weights=w1, own_f32=True, side_weights=w_mix)
        xp, hp, pp, w2_bf = mix(xp.reshape(bp, tp, d), weights=w_mix_bf, side_weights=w2, start_pos=0)
        xs, hs, ps, _ = mix(xs.reshape(bs, ts, d), weights=w_mix_bf, state=state_hgrn[l],
                            hist=state_pool[l], start_pos=past_len)
        xp, xs, _ = ffn2(xp.reshape(bp * tp, d), xs.reshape(bs * ts, d), weights=w2_bf,
                         p=p_prompt[l].reshape(bp * tp, -1), p_tail=p_sample[l].reshape(bs * ts, -1))
        hp_list.append(hp)
        bp_list.append(pp)
        hs_list.append(hs)
        bs_list.append(ps)
    return (xp.reshape(bp, tp, d), xs.reshape(bs, ts, d), jnp.stack(hp_list), jnp.stack(bp_list),
            jnp.stack(hs_list), jnp.stack(bs_list))
```
